```python
import math
import jax, jax.numpy as jnp
from jax import lax
import numpy as np

D_MODEL = 1024
BATCH = 4
SEQ = 8192
DEPTH = 4

N_MIXERS = 3
N_MEM = 256
D_FF = 2816
EPS = 1e-5
Q_BLOCK = 128
ROPE_THETA = 10000.0
DIFF_HEADS = 8
DIFF_DH = D_MODEL // DIFF_HEADS // 2
SB_HEADS = 16
SB_DH = D_MODEL // SB_HEADS
FOX_HEADS = 16
FOX_DH = D_MODEL // FOX_HEADS
XATTN_HEADS = 4
XATTN_DH = D_MODEL // XATTN_HEADS
N_A = (DEPTH + 2) // 3
N_B = (DEPTH + 1) // 3
N_C = DEPTH // 3

kernel_name = "hybrid_diff_stickbreak_fox_macaron"

F32 = jnp.float32


def rms_norm(x, g):
    x32 = x.astype(F32)
    y = x32 * lax.rsqrt(jnp.mean(x32 * x32, axis=-1, keepdims=True) + EPS)
    return (y * g.astype(F32)).astype(x.dtype)


def swiglu(h, w_in, w_out):
    g, u = jnp.split(h @ w_in, 2, axis=-1)
    return (jax.nn.silu(g) * u) @ w_out


def rope_tables(seq, dh):
    inv = 1.0 / (ROPE_THETA ** (jnp.arange(0, dh, 2, dtype=F32) / dh))
    ang = jnp.arange(seq, dtype=F32)[:, None] * inv[None, :]
    return jnp.cos(ang), jnp.sin(ang)


def apply_rope(x, cos, sin):
    x1, x2 = jnp.split(x.astype(F32), 2, axis=-1)
    c = cos[None, :, None, :]
    s = sin[None, :, None, :]
    return jnp.concatenate([x1 * c - x2 * s, x2 * c + x1 * s], axis=-1).astype(x.dtype)


def causal_blocks(block_fn, q_side, k_side):
    seq = q_side[0].shape[2]
    outs = []
    for i in range(seq // Q_BLOCK):
        start, end = i * Q_BLOCK, (i + 1) * Q_BLOCK
        q_pos = jnp.arange(start, end)
        k_pos = jnp.arange(end)
        qs = tuple(t[:, :, start:end] for t in q_side)
        ks = tuple(t[:, :, :end] for t in k_side)
        outs.append(block_fn(qs, ks, q_pos, k_pos))
    return jnp.concatenate(outs, axis=2)


def diff_attention(h, w_qkv, lam, subln, w_o, lambda_init, cos, sin):
    B, S, _ = h.shape
    q, k, v = jnp.split(h @ w_qkv, 3, axis=-1)
    q = apply_rope(q.reshape(B, S, 2 * DIFF_HEADS, DIFF_DH), cos, sin).transpose(0, 2, 1, 3)
    k = apply_rope(k.reshape(B, S, 2 * DIFF_HEADS, DIFF_DH), cos, sin).transpose(0, 2, 1, 3)
    v = v.reshape(B, S, DIFF_HEADS, 2 * DIFF_DH).transpose(0, 2, 1, 3)
    lam32 = lam.astype(F32)
    lambda_full = (jnp.exp(jnp.sum(lam32[0] * lam32[1])) - jnp.exp(jnp.sum(lam32[2] * lam32[3]))
                   + lambda_init)
    scale = DIFF_DH ** -0.5

    def block(qs, ks, q_pos, k_pos):
        (qb,), (kb, vb) = qs, ks
        s = jnp.einsum('bhqd,bhkd->bhqk', qb, kb, preferred_element_type=F32) * scale
        s = jnp.where(k_pos[None, :] <= q_pos[:, None], s, -jnp.inf)
        p = jax.nn.softmax(s, axis=-1)
        p = p.reshape(B, DIFF_HEADS, 2, p.shape[2], p.shape[3])
        a = p[:, :, 0] - lambda_full * p[:, :, 1]
        return jnp.einsum('bhqk,bhkd->bhqd', a.astype(vb.dtype), vb)

    o = causal_blocks(block, (q,), (k, v))
    o = rms_norm(o, subln) * (1.0 - lambda_init)
    return o.transpose(0, 2, 1, 3).reshape(B, S, D_MODEL) @ w_o


def stick_breaking_attention(h, w_qkv, w_o):
    B, S, _ = h.shape
    q, k, v = jnp.split(h @ w_qkv, 3, axis=-1)
    q = q.reshape(B, S, SB_HEADS, SB_DH).transpose(0, 2, 1, 3)
    k = k.reshape(B, S, SB_HEADS, SB_DH).transpose(0, 2, 1, 3)
    v = v.reshape(B, S, SB_HEADS, SB_DH).transpose(0, 2, 1, 3)
    scale = SB_DH ** -0.5

    def block(qs, ks, q_pos, k_pos):
        (qb,), (kb, vb) = qs, ks
        z = jnp.einsum('bhqd,bhkd->bhqk', qb, kb, preferred_element_type=F32) * scale
        valid = k_pos[None, :] < q_pos[:, None]
        log_beta = jax.nn.log_sigmoid(z)
        log_one_minus = jnp.where(valid, jax.nn.log_sigmoid(-z), 0.0)
        suffix = lax.cumsum(log_one_minus, axis=3, reverse=True) - log_one_minus
        a = jnp.where(valid, jnp.exp(log_beta + suffix), 0.0)
        return jnp.einsum('bhqk,bhkd->bhqd', a.astype(vb.dtype), vb)

    o = causal_blocks(block, (q,), (k, v))
    return o.transpose(0, 2, 1, 3).reshape(B, S, D_MODEL) @ w_o


def forgetting_attention(h, w_qkvf, b_f, w_o):
    B, S, _ = h.shape
    proj = h @ w_qkvf
    q = proj[..., :D_MODEL].reshape(B, S, FOX_HEADS, FOX_DH).transpose(0, 2, 1, 3)
    k = proj[..., D_MODEL:2 * D_MODEL].reshape(B, S, FOX_HEADS, FOX_DH).transpose(0, 2, 1, 3)
    v = proj[..., 2 * D_MODEL:3 * D_MODEL].reshape(B, S, FOX_HEADS, FOX_DH).transpose(0, 2, 1, 3)
    log_f = jax.nn.log_sigmoid((proj[..., 3 * D_MODEL:] + b_f).astype(F32))
    c = jnp.cumsum(log_f, axis=1).transpose(0, 2, 1)
    scale = FOX_DH ** -0.5

    def block(qs, ks, q_pos, k_pos):
        (qb, cq), (kb, vb, ck) = qs, ks
        s = jnp.einsum('bhqd,bhkd->bhqk', qb, kb, preferred_element_type=F32) * scale
        s = s + cq[..., :, None] - ck[..., None, :]
        s = jnp.where(k_pos[None, :] <= q_pos[:, None], s, -jnp.inf)
        p = jax.nn.softmax(s, axis=-1)
        return jnp.einsum('bhqk,bhkd->bhqd', p.astype(vb.dtype), vb)

    o = causal_blocks(block, (q, c), (k, v, c))
    return o.transpose(0, 2, 1, 3).reshape(B, S, D_MODEL) @ w_o


def memory_cross_attention(h, mem_n, w_q, w_kv, w_o):
    B, S, _ = h.shape
    M = mem_n.shape[1]
    q = (h @ w_q).reshape(B, S, XATTN_HEADS, XATTN_DH)
    k, v = jnp.split(mem_n @ w_kv, 2, axis=-1)
    k = k.reshape(B, M, XATTN_HEADS, XATTN_DH)
    v = v.reshape(B, M, XATTN_HEADS, XATTN_DH)
    s = jnp.einsum('bshd,bmhd->bhsm', q, k, preferred_element_type=F32) * (XATTN_DH ** -0.5)
    p = jax.nn.softmax(s, axis=-1)
    o = jnp.einsum('bhsm,bmhd->bshd', p.astype(v.dtype), v).reshape(B, S, D_MODEL)
    return o @ w_o


def setup_inputs(seed: int = 0) -> dict:
    key = jax.random.key(seed)
    ks = jax.random.split(key, 24)

    def nrm(k, shape, fan_in, mult=1.0):
        return jax.random.normal(k, shape, F32) * (mult * fan_in ** -0.5)

    def gain(k, shape):
        return 1.0 + 0.02 * jax.random.normal(k, shape, F32)

    D, F = D_MODEL, D_FF
    return {
        "x": jax.random.normal(ks[0], (BATCH, SEQ, D), F32),
        "mem": jax.random.normal(ks[1], (BATCH, N_MEM, D), F32),
        "ffn_norm": gain(ks[2], (DEPTH, 2, D)),
        "ffn_w_in": nrm(ks[3], (DEPTH, 2, D, 2 * F), D),
        "ffn_w_out": nrm(ks[4], (DEPTH, 2, F, D), F),
        "mix_norm": gain(ks[5], (DEPTH, D)),
        "diff_w_qkv": nrm(ks[6], (N_A, D, 3 * D), D),
        "diff_lambda": 0.1 * jax.random.normal(ks[7], (N_A, 4, DIFF_DH), F32),
        "diff_subln": gain(ks[8], (N_A, 2 * DIFF_DH)),
        "diff_w_o": nrm(ks[9], (N_A, D, D), D),
        "sb_w_qkv": nrm(ks[10], (N_B, D, 3 * D), D),
        "sb_w_o": nrm(ks[11], (N_B, D, D), D),
        "fox_w_qkvf": nrm(ks[12], (N_C, D, 3 * D + FOX_HEADS), D),
        "fox_b_f": jax.random.uniform(ks[13], (N_C, FOX_HEADS), F32, 1.0, 4.0),
        "fox_w_o": nrm(ks[14], (N_C, D, D), D),
        "xattn_norm": gain(ks[15], (DEPTH, D)),
        "mem_norm": gain(ks[16], (D,)),
        "xattn_w_q": nrm(ks[17], (DEPTH, D, D), D),
        "xattn_w_kv": nrm(ks[18], (DEPTH, D, 2 * D), D),
        "xattn_w_o": nrm(ks[19], (DEPTH, D, D), D),
        "final_norm": gain(ks[20], (D,)),
    }


def reference(x, mem, ffn_norm, ffn_w_in, ffn_w_out, mix_norm,
              diff_w_qkv, diff_lambda, diff_subln, diff_w_o,
              sb_w_qkv, sb_w_o, fox_w_qkvf, fox_b_f, fox_w_o,
              xattn_norm, mem_norm, xattn_w_q, xattn_w_kv, xattn_w_o, final_norm):
    S = x.shape[1]
    cos, sin = rope_tables(S, DIFF_DH)
    mem_n = rms_norm(mem, mem_norm)
    for i in range(DEPTH):
        mixer, j = i % N_MIXERS, i // N_MIXERS
        x = x + 0.5 * swiglu(rms_norm(x, ffn_norm[i, 0]), ffn_w_in[i, 0], ffn_w_out[i, 0])
        h = rms_norm(x, mix_norm[i])
        if mixer == 0:
            lambda_init = 0.8 - 0.6 * math.exp(-0.3 * i)
            y = diff_attention(h, diff_w_qkv[j], diff_lambda[j], diff_subln[j], diff_w_o[j],
                               lambda_init, cos, sin)
        elif mixer == 1:
            y = stick_breaking_attention(h, sb_w_qkv[j], sb_w_o[j])
        else:
            y = forgetting_attention(h, fox_w_qkvf[j], fox_b_f[j], fox_w_o[j])
        x = x + y
        x = x + memory_cross_attention(rms_norm(x, xattn_norm[i]), mem_n,
                                       xattn_w_q[i], xattn_w_kv[i], xattn_w_o[i])
        x = x + 0.5 * swiglu(rms_norm(x, ffn_norm[i, 1]), ffn_w_in[i, 1], ffn_w_out[i, 1])
    return rms_norm(x, final_norm)
```

```python
import functools
import math

import jax
import jax.numpy as jnp
from jax import lax
from jax.experimental import pallas as pl
from jax.experimental.pallas import tpu as pltpu

F32 = jnp.float32
BF16 = jnp.bfloat16

EPS = 1e-5
ROPE_THETA = 10000.0
N_MIXERS = 3
HEAD_DIM = 64
XATTN_HEADS = 4
LANES = 128
VMEM_LIMIT_BYTES = 48 * 1024 * 1024

ROW_TILE = 1024
FFN_TILE = 256
ATTN_TILE = 256
GATE_TILE = 256


def _params(*semantics):
    return pltpu.CompilerParams(dimension_semantics=semantics,
                                vmem_limit_bytes=VMEM_LIMIT_BYTES)


def _rms(x, g):
    var = jnp.mean(x * x, axis=-1, keepdims=True)
    return x * lax.rsqrt(var + EPS) * g


def _dot(a, b):
    return jnp.dot(a, b, preferred_element_type=F32)


def _dot_nt(a, b):
    return lax.dot_general(a, b, (((1,), (1,)), ((), ())), preferred_element_type=F32)


def _softplus(z):
    return jnp.maximum(z, 0.0) + jnp.log(1.0 + jnp.exp(-jnp.abs(z)))


def _rope_swap(y):
    lane = lax.broadcasted_iota(jnp.int32, y.shape, 1)
    first_half = (lane % HEAD_DIM) < (HEAD_DIM // 2)
    return jnp.where(first_half, pltpu.roll(y, LANES - HEAD_DIM // 2, axis=1),
                     pltpu.roll(y, HEAD_DIM // 2, axis=1))


def _norm_matmul_kernel(x_ref, g_ref, w_ref, *rest, rope_blocks, scale_blocks, scale):
    if rope_blocks:
        cos_ref, sin_ref, o_ref, h_scr = rest
    else:
        o_ref, h_scr = rest
    n = pl.program_id(1)

    @pl.when(n == 0)
    def _():
        h_scr[...] = _rms(x_ref[...], g_ref[...]).astype(BF16)

    y = _dot(h_scr[...], w_ref[...])
    if scale_blocks:
        y = y * jnp.where(n < scale_blocks, scale, 1.0).astype(F32)

    if rope_blocks:
        @pl.when(n < rope_blocks)
        def _():
            cos = cos_ref[...]
            sin = sin_ref[...]
            for c in range(y.shape[1] // LANES):
                yc = y[:, c * LANES:(c + 1) * LANES]
                o_ref[:, c * LANES:(c + 1) * LANES] = (
                    yc * cos + _rope_swap(yc) * sin).astype(o_ref.dtype)

        @pl.when(n >= rope_blocks)
        def _():
            o_ref[...] = y.astype(o_ref.dtype)
    else:
        o_ref[...] = y.astype(o_ref.dtype)


def _norm_matmul(x, gain, w, *, out_dtype, tn, rope=None, rope_blocks=0,
                 scale_blocks=0, scale=1.0, seq=None):
    m, d = x.shape
    n_cols = w.shape[1]
    tm = min(ROW_TILE, seq if rope_blocks else m)
    in_specs = [
        pl.BlockSpec((tm, d), lambda i, n: (i, 0)),
        pl.BlockSpec((1, d), lambda i, n: (0, 0)),
        pl.BlockSpec((d, tn), lambda i, n: (0, n)),
    ]
    args = [x, gain.reshape(1, d), w]
    if rope_blocks:
        tiles_per_seq = seq // tm
        in_specs += [pl.BlockSpec((tm, LANES), lambda i, n: (i % tiles_per_seq, 0))] * 2
        args += list(rope)
    return pl.pallas_call(
        functools.partial(_norm_matmul_kernel, rope_blocks=rope_blocks,
                          scale_blocks=scale_blocks, scale=scale),
        grid=(m // tm, n_cols // tn),
        in_specs=in_specs,
        out_specs=pl.BlockSpec((tm, tn), lambda i, n: (i, n)),
        out_shape=jax.ShapeDtypeStruct((m, n_cols), out_dtype),
        scratch_shapes=[pltpu.VMEM((tm, d), BF16)],
        compiler_params=_params("parallel", "arbitrary"),
        name="norm_matmul",
    )(*args)


def _ffn_kernel(x_ref, g_ref, wg_ref, wu_ref, wo_ref, *rest, final_norm):
    if final_norm:
        fg_ref, o_ref, h_scr, acc_scr = rest
    else:
        o_ref, h_scr, acc_scr = rest
    f = pl.program_id(1)

    @pl.when(f == 0)
    def _():
        h_scr[...] = _rms(x_ref[...], g_ref[...]).astype(BF16)
        acc_scr[...] = jnp.zeros_like(acc_scr)

    h = h_scr[...]
    gate = _dot(h, wg_ref[...])
    up = _dot(h, wu_ref[...])
    act = (gate * jax.nn.sigmoid(gate) * up).astype(BF16)
    acc_scr[...] += _dot(act, wo_ref[...])

    @pl.when(f == pl.num_programs(1) - 1)
    def _():
        y = x_ref[...] + 0.5 * acc_scr[...]
        if final_norm:
            y = _rms(y, fg_ref[...])
        o_ref[...] = y


def _ffn(x, gain, w_in, w_out, final_gain=None):
    m, d = x.shape
    d_ff = w_out.shape[0]
    tm = min(ROW_TILE, m)
    tf = FFN_TILE
    nf = d_ff // tf
    in_specs = [
        pl.BlockSpec((tm, d), lambda i, f: (i, 0)),
        pl.BlockSpec((1, d), lambda i, f: (0, 0)),
        pl.BlockSpec((d, tf), lambda i, f: (0, f)),
        pl.BlockSpec((d, tf), lambda i, f: (0, nf + f)),
        pl.BlockSpec((tf, d), lambda i, f: (f, 0)),
    ]
    args = [x, gain.reshape(1, d), w_in, w_in, w_out]
    if final_gain is not None:
        in_specs.append(pl.BlockSpec((1, d), lambda i, f: (0, 0)))
        args.append(final_gain.reshape(1, d))
    return pl.pallas_call(
        functools.partial(_ffn_kernel, final_norm=final_gain is not None),
        grid=(m // tm, nf),
        in_specs=in_specs,
        out_specs=pl.BlockSpec((tm, d), lambda i, f: (i, 0)),
        out_shape=jax.ShapeDtypeStruct((m, d), F32),
        scratch_shapes=[pltpu.VMEM((tm, d), BF16), pltpu.VMEM((tm, d), F32)],
        compiler_params=_params("parallel", "arbitrary"),
        name="ffn",
    )(*args)


def _fox_gate_kernel(z_ref, b_ref, ccol_ref, crow_ref, carry_scr, *, n_heads):
    @pl.when(pl.program_id(1) == 0)
    def _():
        carry_scr[...] = jnp.zeros_like(carry_scr)

    z = z_ref[0] + b_ref[...]
    log_f = -_softplus(-z)
    ts = log_f.shape[0]
    row = lax.broadcasted_iota(jnp.int32, (ts, ts), 0)
    col = lax.broadcasted_iota(jnp.int32, (ts, ts), 1)
    tri = jnp.where(row >= col, 1.0, 0.0).astype(BF16)
    hi = log_f.astype(BF16)
    r1 = log_f - hi.astype(F32)
    mid = r1.astype(BF16)
    lo = (r1 - mid.astype(F32)).astype(BF16)
    c = _dot(tri, hi) + _dot(tri, mid) + _dot(tri, lo) + carry_scr[...]
    ccol_ref[0] = c
    crow_ref[0] = c.T[:n_heads]
    carry_scr[...] = c[ts - 1:ts, :]


def _fox_gate(z, b_pad, n_heads):
    bsz, seq, _ = z.shape
    ts = min(GATE_TILE, seq)
    return pl.pallas_call(
        functools.partial(_fox_gate_kernel, n_heads=n_heads),
        grid=(bsz, seq // ts),
        in_specs=[pl.BlockSpec((1, ts, LANES), lambda b, t: (b, t, 0)),
                  pl.BlockSpec((1, LANES), lambda b, t: (0, 0))],
        out_specs=[pl.BlockSpec((1, ts, LANES), lambda b, t: (b, t, 0)),
                   pl.BlockSpec((1, n_heads, ts), lambda b, t: (b, 0, t))],
        out_shape=[jax.ShapeDtypeStruct((bsz, seq, LANES), F32),
                   jax.ShapeDtypeStruct((bsz, n_heads, seq), F32)],
        scratch_shapes=[pltpu.VMEM((1, LANES), F32)],
        compiler_params=_params("parallel", "arbitrary"),
        name="fox_gate",
    )(z, b_pad)


def _stack_heads(q):
    lane = lax.broadcasted_iota(jnp.int32, q.shape, 1)
    zero = jnp.zeros_like(q)
    return jnp.concatenate([jnp.where(lane < HEAD_DIM, q, zero),
                            jnp.where(lane >= HEAD_DIM, q, zero)], axis=0)


def _attn_kernel(*refs, mode, t, lambda_init):
    if mode == "diff":
        q_ref, k_ref, v_ref, lam_ref, subln_ref, o_ref, acc_scr, m_scr, l_scr = refs
    elif mode == "fox":
        q_ref, k_ref, v_ref, ccol_ref, crow_ref, o_ref, acc_scr, m_scr, l_scr = refs
    else:
        q_ref, k_ref, v_ref, o_ref, acc_scr, r_scr = refs
    hp = pl.program_id(1)
    qi = pl.program_id(2)

    q2 = _stack_heads(q_ref[0])
    row = lax.broadcasted_iota(jnp.int32, (2 * t, t), 0)
    col = lax.broadcasted_iota(jnp.int32, (2 * t, t), 1)
    q_in_tile = jnp.where(row >= t, row - t, row)
    acc_scr[...] = jnp.zeros_like(acc_scr)

    def kv_tile(j):
        start = pl.multiple_of(j * t, t)
        return k_ref[0, pl.ds(start, t), :], v_ref[0, pl.ds(start, t), :]

    if mode == "sb":
        strictly_lower = jnp.where(
            lax.broadcasted_iota(jnp.int32, (t, t), 0) > lax.broadcasted_iota(jnp.int32, (t, t), 1),
            1.0, 0.0).astype(BF16)
        r_scr[...] = jnp.zeros_like(r_scr)

        def sb_tile(j, diagonal):
            k, v = kv_tile(j)
            z = _dot_nt(q2, k)
            sp = _softplus(z)
            log_beta = z - sp
            log_rest = -sp
            if diagonal:
                valid = col < q_in_tile
                log_rest = jnp.where(valid, log_rest, 0.0)
            hi = log_rest.astype(BF16)
            lo = (log_rest - hi.astype(F32)).astype(BF16)
            suffix = _dot(hi, strictly_lower) + _dot(lo, strictly_lower)
            a = jnp.exp(log_beta + suffix + r_scr[...])
            if diagonal:
                a = jnp.where(valid, a, 0.0)
            acc_scr[...] += _dot(a.astype(BF16), v)
            r_scr[...] += jnp.sum(log_rest, axis=-1, keepdims=True)

        sb_tile(qi, True)

        def body(step, carry):
            sb_tile(qi - 1 - step, False)
            return carry
        lax.fori_loop(0, qi, body, 0)
        acc = acc_scr[...]
        lane = lax.broadcasted_iota(jnp.int32, (t, LANES), 1)
        o_ref[0] = jnp.where(lane < HEAD_DIM, acc[:t], acc[t:]).astype(o_ref.dtype)
        return

    m_scr[...] = jnp.full_like(m_scr, -jnp.inf)
    l_scr[...] = jnp.zeros_like(l_scr)
    if mode == "fox":
        cc = ccol_ref[0]
        lane = lax.broadcasted_iota(jnp.int32, cc.shape, 1)
        cq = jnp.concatenate(
            [jnp.sum(jnp.where(lane == 2 * hp, cc, 0.0), axis=-1, keepdims=True),
             jnp.sum(jnp.where(lane == 2 * hp + 1, cc, 0.0), axis=-1, keepdims=True)], axis=0)

    def softmax_tile(j, diagonal):
        k, v = kv_tile(j)
        s = _dot_nt(q2, k)
        if mode == "fox":
            start = pl.multiple_of(j * t, t)
            ck = crow_ref[0, 0, :, pl.ds(start, t)]
            ck2 = jnp.concatenate([jnp.broadcast_to(ck[0:1], (t, t)),
                                   jnp.broadcast_to(ck[1:2], (t, t))], axis=0)
            s = s + (cq - ck2)
        if diagonal:
            s = jnp.where(col <= q_in_tile, s, -jnp.inf)
        m_old = m_scr[...]
        m_new = jnp.maximum(m_old, jnp.max(s, axis=-1, keepdims=True))
        alpha = jnp.exp(m_old - m_new)
        p = jnp.exp(s - m_new)
        l_scr[...] = alpha * l_scr[...] + jnp.sum(p, axis=-1, keepdims=True)
        acc_scr[...] = alpha * acc_scr[...] + _dot(p.astype(BF16), v)
        m_scr[...] = m_new

    def body(j, carry):
        softmax_tile(j, False)
        return carry
    lax.fori_loop(0, qi, body, 0)
    softmax_tile(qi, True)

    o = acc_scr[...] / l_scr[...]
    if mode == "fox":
        lane = lax.broadcasted_iota(jnp.int32, (t, LANES), 1)
        o_ref[0] = jnp.where(lane < HEAD_DIM, o[:t], o[t:]).astype(o_ref.dtype)
    else:
        lam = lam_ref[...]
        lam_full = (jnp.exp(jnp.sum(lam[0:1] * lam[1:2], axis=-1, keepdims=True))
                    - jnp.exp(jnp.sum(lam[2:3] * lam[3:4], axis=-1, keepdims=True))
                    + lambda_init)
        d = o[:t] - lam_full * o[t:]
        o_ref[0] = (_rms(d, subln_ref[...]) * (1.0 - lambda_init)).astype(o_ref.dtype)


def _attention(qkv, mode, *, bsz, seq, d_model, extra=(), lambda_init=0.0):
    t = min(ATTN_TILE, seq)
    n_pairs = d_model // LANES
    in_specs = [
        pl.BlockSpec((1, t, LANES), lambda b, h, i: (b, i, h)),
        pl.BlockSpec((1, seq, LANES), lambda b, h, i: (b, 0, n_pairs + h)),
        pl.BlockSpec((1, seq, LANES), lambda b, h, i: (b, 0, 2 * n_pairs + h)),
    ]
    scratch = [pltpu.VMEM((2 * t, LANES), F32)]
    if mode == "diff":
        in_specs += [pl.BlockSpec((4, HEAD_DIM), lambda b, h, i: (0, 0)),
                     pl.BlockSpec((1, LANES), lambda b, h, i: (0, 0))]
        scratch += [pltpu.VMEM((2 * t, 1), F32)] * 2
    elif mode == "fox":
        in_specs += [pl.BlockSpec((1, t, LANES), lambda b, h, i: (b, i, 0)),
                     pl.BlockSpec((1, 1, 2, seq), lambda b, h, i: (b, h, 0, 0))]
        scratch += [pltpu.VMEM((2 * t, 1), F32)] * 2
    else:
        scratch += [pltpu.VMEM((2 * t, 1), F32)]
    return pl.pallas_call(
        functools.partial(_attn_kernel, mode=mode, t=t, lambda_init=lambda_init),
        grid=(bsz, n_pairs, seq // t),
        in_specs=in_specs,
        out_specs=pl.BlockSpec((1, t, LANES), lambda b, h, i: (b, i, h)),
        out_shape=jax.ShapeDtypeStruct((bsz, seq, d_model), BF16),
        scratch_shapes=scratch,
        compiler_params=_params("parallel", "parallel", "arbitrary"),
        name="attn_" + mode,
    )(qkv, qkv, qkv, *extra)


def _outproj_xattn_kernel(x_ref, o_ref, wo_ref, g_ref, wq_ref, kv_ref, wxo_ref, out_ref):
    d = x_ref.shape[1]
    dh = d // XATTN_HEADS
    x1 = x_ref[...] + _dot(o_ref[...], wo_ref[...])
    h = _rms(x1, g_ref[...]).astype(BF16)
    q = (_dot(h, wq_ref[...]) * (dh ** -0.5)).astype(BF16)
    heads = []
    for hd in range(XATTN_HEADS):
        k = kv_ref[0, :, hd * dh:(hd + 1) * dh]
        v = kv_ref[0, :, d + hd * dh:d + (hd + 1) * dh]
        s = _dot_nt(q[:, hd * dh:(hd + 1) * dh], k)
        p = jnp.exp(s - jnp.max(s, axis=-1, keepdims=True))
        inv = 1.0 / jnp.sum(p, axis=-1, keepdims=True)
        heads.append((_dot(p.astype(BF16), v) * inv).astype(BF16))
    out_ref[...] = x1 + _dot(jnp.concatenate(heads, axis=1), wxo_ref[...])


def _outproj_xattn(x, o, w_o, gain, w_q, kv, w_xo, *, seq):
    m, d = x.shape
    n_mem = kv.shape[1]
    tm = min(ROW_TILE // 2, seq)
    tiles_per_seq = seq // tm
    full = lambda i: (0, 0)
    return pl.pallas_call(
        _outproj_xattn_kernel,
        grid=(m // tm,),
        in_specs=[
            pl.BlockSpec((tm, d), lambda i: (i, 0)),
            pl.BlockSpec((tm, d), lambda i: (i, 0)),
            pl.BlockSpec((d, d), full),
            pl.BlockSpec((1, d), full),
            pl.BlockSpec((d, d), full),
            pl.BlockSpec((1, n_mem, 2 * d), lambda i: (i // tiles_per_seq, 0, 0)),
            pl.BlockSpec((d, d), full),
        ],
        out_specs=pl.BlockSpec((tm, d), lambda i: (i, 0)),
        out_shape=jax.ShapeDtypeStruct((m, d), F32),
        compiler_params=_params("parallel"),
        name="outproj_xattn",
    )(x, o, w_o, gain.reshape(1, d), w_q, kv, w_xo)


def _rope_tables(seq):
    half = HEAD_DIM // 2
    inv = 1.0 / (ROPE_THETA ** (jnp.arange(0, HEAD_DIM, 2, dtype=F32) / HEAD_DIM))
    ang = jnp.arange(seq, dtype=F32)[:, None] * inv[None, :]
    cos, sin = jnp.cos(ang), jnp.sin(ang)
    reps = LANES // HEAD_DIM
    return (jnp.tile(jnp.concatenate([cos, cos], axis=1), (1, reps)),
            jnp.tile(jnp.concatenate([-sin, sin], axis=1), (1, reps)))


def kernel(x, mem, ffn_norm, ffn_w_in, ffn_w_out, mix_norm, diff_w_qkv, diff_lambda, diff_subln, diff_w_o, sb_w_qkv, sb_w_o, fox_w_qkvf, fox_b_f, fox_w_o, xattn_norm, mem_norm, xattn_w_q, xattn_w_kv, xattn_w_o, final_norm):
    bsz, seq, d = x.shape
    n_mem = mem.shape[1]
    depth = ffn_norm.shape[0]
    n_heads = d // HEAD_DIM
    m = bsz * seq
    assert d % LANES == 0 and seq % min(ATTN_TILE, seq) == 0 and seq % min(ROW_TILE, seq) == 0
    rope = _rope_tables(seq)
    qk_scale = HEAD_DIM ** -0.5
    bf = lambda w: w.astype(BF16)

    xf = x.reshape(m, d)
    memf = mem.reshape(bsz * n_mem, d)
    for i in range(depth):
        mixer, j = i % N_MIXERS, i // N_MIXERS
        xf = _ffn(xf, ffn_norm[i, 0], bf(ffn_w_in[i, 0]), bf(ffn_w_out[i, 0]))

        if mixer == 0:
            lambda_init = 0.8 - 0.6 * math.exp(-0.3 * i)
            qkv = _norm_matmul(xf, mix_norm[i], bf(diff_w_qkv[j]), out_dtype=BF16, tn=d,
                               rope=rope, rope_blocks=2, scale_blocks=1, scale=qk_scale, seq=seq)
            o = _attention(qkv.reshape(bsz, seq, 3 * d), "diff", bsz=bsz, seq=seq, d_model=d,
                           extra=(diff_lambda[j], diff_subln[j].reshape(1, LANES)),
                           lambda_init=lambda_init)
            w_o = diff_w_o[j]
        elif mixer == 1:
            qkv = _norm_matmul(xf, mix_norm[i], bf(sb_w_qkv[j]), out_dtype=BF16, tn=d,
                               scale_blocks=1, scale=qk_scale)
            o = _attention(qkv.reshape(bsz, seq, 3 * d), "sb", bsz=bsz, seq=seq, d_model=d)
            w_o = sb_w_o[j]
        else:
            w = fox_w_qkvf[j]
            qkv = _norm_matmul(xf, mix_norm[i], bf(w[:, :3 * d]), out_dtype=BF16, tn=d,
                               scale_blocks=1, scale=qk_scale)
            w_gate = jnp.pad(bf(w[:, 3 * d:]), ((0, 0), (0, LANES - n_heads)))
            z = _norm_matmul(xf, mix_norm[i], w_gate, out_dtype=F32, tn=LANES)
            b_pad = jnp.pad(fox_b_f[j], (0, LANES - n_heads)).reshape(1, LANES)
            c_col, c_row = _fox_gate(z.reshape(bsz, seq, LANES), b_pad, n_heads)
            o = _attention(qkv.reshape(bsz, seq, 3 * d), "fox", bsz=bsz, seq=seq, d_model=d,
                           extra=(c_col, c_row.reshape(bsz, n_heads // 2, 2, seq)))
            w_o = fox_w_o[j]

        kv = _norm_matmul(memf, mem_norm, bf(xattn_w_kv[i]), out_dtype=BF16, tn=d)
        xf = _outproj_xattn(xf, o.reshape(m, d), bf(w_o), xattn_norm[i], bf(xattn_w_q[i]),
                            kv.reshape(bsz, n_mem, 2 * d), bf(xattn_w_o[i]), seq=seq)

        xf = _ffn(xf, ffn_norm[i, 1], bf(ffn_w_in[i, 1]), bf(ffn_w_out[i, 1]),
                  final_gain=final_norm if i == depth - 1 else None)
    return xf.reshape(bsz, seq, d)
```

```python
import functools
import math

import numpy as np
import jax
import jax.numpy as jnp
from jax import lax
from jax.experimental import pallas as pl
from jax.experimental.pallas import tpu as pltpu

F32 = jnp.float32
BF16 = jnp.bfloat16

EPS = 1e-5
ROPE_THETA = 10000.0
N_MIXERS = 3
HEAD_DIM = 64
XATTN_HEADS = 4
LANES = 128
VMEM_LIMIT_BYTES = 48 * 1024 * 1024

ROW_TILE = 1024
FFN_TILE = 256
ATTN_Q_TILE = 256
ATTN_KV_STEP = 1024
GATE_TILE = 256
N_SPLIT = 3


def _params(*semantics):
    return pltpu.CompilerParams(dimension_semantics=semantics,
                                vmem_limit_bytes=VMEM_LIMIT_BYTES)


def _rms(x, g):
    var = jnp.mean(x * x, axis=-1, keepdims=True)
    return x * lax.rsqrt(var + EPS) * g


def _dot(a, b):
    return jnp.dot(a, b, preferred_element_type=F32)


def _dot_nt(a, b):
    return lax.dot_general(a, b, (((1,), (1,)), ((), ())), preferred_element_type=F32)


def _softplus(z):
    return jnp.maximum(z, 0.0) + jnp.log(1.0 + jnp.exp(-jnp.abs(z)))


def _split3(x):
    hi = x.astype(BF16)
    r1 = x - hi.astype(F32)
    mid = r1.astype(BF16)
    lo = (r1 - mid.astype(F32)).astype(BF16)
    return hi, mid, lo


def _norm_matmul_kernel(x_ref, g_ref, w_ref, o_ref, h_scr):
    @pl.when(pl.program_id(1) == 0)
    def _():
        h_scr[...] = _rms(x_ref[...], g_ref[...]).astype(BF16)

    o_ref[...] = _dot(h_scr[...], w_ref[...]).astype(o_ref.dtype)


def _norm_matmul(x, gain, w, *, out_dtype, tn):
    m, d = x.shape
    n_cols = w.shape[1]
    tm = min(ROW_TILE, m)
    return pl.pallas_call(
        _norm_matmul_kernel,
        grid=(m // tm, n_cols // tn),
        in_specs=[pl.BlockSpec((tm, d), lambda i, n: (i, 0)),
                  pl.BlockSpec((1, d), lambda i, n: (0, 0)),
                  pl.BlockSpec((d, tn), lambda i, n: (0, n))],
        out_specs=pl.BlockSpec((tm, tn), lambda i, n: (i, n)),
        out_shape=jax.ShapeDtypeStruct((m, n_cols), out_dtype),
        scratch_shapes=[pltpu.VMEM((tm, d), BF16)],
        compiler_params=_params("parallel", "arbitrary"),
        name="norm_matmul",
    )(x, gain.reshape(1, d), w)


def _rope_swap(y):
    lane = lax.broadcasted_iota(jnp.int32, y.shape, 1)
    first_half = (lane % HEAD_DIM) < (HEAD_DIM // 2)
    return jnp.where(first_half, pltpu.roll(y, LANES - HEAD_DIM // 2, axis=1),
                     pltpu.roll(y, HEAD_DIM // 2, axis=1))


def _qkv_kernel(x_ref, g_ref, wqt_ref, wk_ref, wvt_ref, *rest, rope, scale):
    if rope:
        cost_ref, sint_ref, cos_ref, sin_ref, qt_ref, k_ref, vt_ref = rest
    else:
        qt_ref, k_ref, vt_ref = rest
    h = _rms(x_ref[...], g_ref[...]).astype(BF16)
    qt = _dot_nt(wqt_ref[...], h) * scale
    k = _dot(h, wk_ref[...])
    if rope:
        half = HEAD_DIM // 2
        ct, st = cost_ref[...], sint_ref[...]
        for r0 in range(0, qt.shape[0], HEAD_DIM):
            x1, x2 = qt[r0:r0 + half], qt[r0 + half:r0 + HEAD_DIM]
            qt_ref[0, r0:r0 + half, :] = (x1 * ct - x2 * st).astype(BF16)
            qt_ref[0, r0 + half:r0 + HEAD_DIM, :] = (x2 * ct + x1 * st).astype(BF16)
        cos, sin = cos_ref[...], sin_ref[...]
        for c0 in range(0, k.shape[1], LANES):
            kc = k[:, c0:c0 + LANES]
            k_ref[:, c0:c0 + LANES] = (kc * cos + _rope_swap(kc) * sin).astype(BF16)
    else:
        qt_ref[0] = qt.astype(BF16)
        k_ref[...] = k.astype(BF16)
    vt_ref[0] = _dot_nt(wvt_ref[...], h).astype(BF16)


def _qkv_proj(x, gain, w_q, w_k, w_v, *, bsz, seq, scale, rope=None):
    m, d = x.shape
    tm = min(ROW_TILE // 2, seq)
    tiles_per_seq = seq // tm
    full = lambda i: (0, 0)
    feat_major = pl.BlockSpec((1, d, tm), lambda i: (i // tiles_per_seq, 0, i % tiles_per_seq))
    in_specs = [pl.BlockSpec((tm, d), lambda i: (i, 0)),
                pl.BlockSpec((1, d), full),
                pl.BlockSpec((d, d), full), pl.BlockSpec((d, d), full), pl.BlockSpec((d, d), full)]
    args = [x, gain.reshape(1, d), w_q.T.astype(BF16), w_k.astype(BF16), w_v.T.astype(BF16)]
    if rope is not None:
        half = HEAD_DIM // 2
        in_specs += [pl.BlockSpec((half, tm), lambda i: (0, i % tiles_per_seq))] * 2
        in_specs += [pl.BlockSpec((tm, LANES), lambda i: (i % tiles_per_seq, 0))] * 2
        args += list(rope)
    return pl.pallas_call(
        functools.partial(_qkv_kernel, rope=rope is not None, scale=scale),
        grid=(m // tm,),
        in_specs=in_specs,
        out_specs=[feat_major, pl.BlockSpec((tm, d), lambda i: (i, 0)), feat_major],
        out_shape=[jax.ShapeDtypeStruct((bsz, d, seq), BF16),
                   jax.ShapeDtypeStruct((m, d), BF16),
                   jax.ShapeDtypeStruct((bsz, d, seq), BF16)],
        compiler_params=_params("parallel"),
        name="qkv_proj",
    )(*args)


def _ffn_kernel(x_ref, g_ref, wg_ref, wu_ref, wo_ref, *rest, final_norm):
    if final_norm:
        fg_ref, o_ref, h_scr, acc_scr = rest
    else:
        o_ref, h_scr, acc_scr = rest
    f = pl.program_id(1)

    @pl.when(f == 0)
    def _():
        h_scr[...] = _rms(x_ref[...], g_ref[...]).astype(BF16)
        acc_scr[...] = jnp.zeros_like(acc_scr)

    h = h_scr[...]
    gate = _dot(h, wg_ref[...])
    up = _dot(h, wu_ref[...])
    act = (gate * jax.nn.sigmoid(gate) * up).astype(BF16)
    acc_scr[...] += _dot(act, wo_ref[...])

    @pl.when(f == pl.num_programs(1) - 1)
    def _():
        y = x_ref[...] + 0.5 * acc_scr[...]
        if final_norm:
            y = _rms(y, fg_ref[...])
        o_ref[...] = y


def _ffn(x, gain, w_in, w_out, final_gain=None):
    m, d = x.shape
    d_ff = w_out.shape[0]
    tm = min(ROW_TILE, m)
    tf = FFN_TILE
    nf = d_ff // tf
    in_specs = [
        pl.BlockSpec((tm, d), lambda i, f: (i, 0)),
        pl.BlockSpec((1, d), lambda i, f: (0, 0)),
        pl.BlockSpec((d, tf), lambda i, f: (0, f)),
        pl.BlockSpec((d, tf), lambda i, f: (0, nf + f)),
        pl.BlockSpec((tf, d), lambda i, f: (f, 0)),
    ]
    args = [x, gain.reshape(1, d), w_in, w_in, w_out]
    if final_gain is not None:
        in_specs.append(pl.BlockSpec((1, d), lambda i, f: (0, 0)))
        args.append(final_gain.reshape(1, d))
    return pl.pallas_call(
        functools.partial(_ffn_kernel, final_norm=final_gain is not None),
        grid=(m // tm, nf),
        in_specs=in_specs,
        out_specs=pl.BlockSpec((tm, d), lambda i, f: (i, 0)),
        out_shape=jax.ShapeDtypeStruct((m, d), F32),
        scratch_shapes=[pltpu.VMEM((tm, d), BF16), pltpu.VMEM((tm, d), F32)],
        compiler_params=_params("parallel", "arbitrary"),
        name="ffn",
    )(*args)


def _fox_placement(n_heads, d):
    place = np.zeros((N_SPLIT, LANES, d), np.float32)
    ones = np.zeros((1, d), np.float32)
    for h in range(n_heads):
        base = (h // 2) * LANES + (h % 2) * N_SPLIT
        for piece in range(N_SPLIT):
            place[piece, h, base + piece] = -1.0
    for pair in range(d // LANES):
        ones[0, pair * LANES + 2 * N_SPLIT:pair * LANES + 3 * N_SPLIT] = 1.0
    return jnp.asarray(place, BF16), jnp.asarray(ones, F32)


def _fox_gate_kernel(z_ref, b_ref, place_ref, ones_ref, crow_ref, ex_ref, carry_scr, *, n_heads):
    @pl.when(pl.program_id(1) == 0)
    def _():
        carry_scr[...] = jnp.zeros_like(carry_scr)

    z = z_ref[0] + b_ref[...]
    log_f = -_softplus(-z)
    ts = log_f.shape[0]
    row = lax.broadcasted_iota(jnp.int32, (ts, ts), 0)
    col = lax.broadcasted_iota(jnp.int32, (ts, ts), 1)
    tri = jnp.where(row >= col, 1.0, 0.0).astype(BF16)
    hi, mid, lo = _split3(log_f)
    c = _dot(tri, hi) + _dot(tri, mid) + _dot(tri, lo) + carry_scr[...]
    carry_scr[...] = c[ts - 1:ts, :]
    crow_ref[0] = c.T[:n_heads]
    hi, mid, lo = _split3(c)
    ex = _dot(hi, place_ref[0]) + _dot(mid, place_ref[1]) + _dot(lo, place_ref[2]) + ones_ref[...]
    ex_ref[0] = ex.astype(BF16)


def _fox_gate(z, b_pad, n_heads, d):
    bsz, seq, _ = z.shape
    ts = min(GATE_TILE, seq)
    place, ones = _fox_placement(n_heads, d)
    return pl.pallas_call(
        functools.partial(_fox_gate_kernel, n_heads=n_heads),
        grid=(bsz, seq // ts),
        in_specs=[pl.BlockSpec((1, ts, LANES), lambda b, t: (b, t, 0)),
                  pl.BlockSpec((1, LANES), lambda b, t: (0, 0)),
                  pl.BlockSpec((N_SPLIT, LANES, d), lambda b, t: (0, 0, 0)),
                  pl.BlockSpec((1, d), lambda b, t: (0, 0))],
        out_specs=[pl.BlockSpec((1, n_heads, ts), lambda b, t: (b, 0, t)),
                   pl.BlockSpec((1, ts, d), lambda b, t: (b, t, 0))],
        out_shape=[jax.ShapeDtypeStruct((bsz, n_heads, seq), F32),
                   jax.ShapeDtypeStruct((bsz, seq, d), BF16)],
        scratch_shapes=[pltpu.VMEM((1, LANES), F32)],
        compiler_params=_params("parallel", "arbitrary"),
        name="fox_gate",
    )(z, b_pad, place, ones)


def _attn_kernel(*refs, mode, t, tk, lambda_init):
    if mode == "diff":
        qt_ref, k_ref, vt_ref, lam_ref, subln_ref, o_ref, acc_scr = refs
    elif mode == "fox":
        qt_ref, k_ref, vt_ref, ex_ref, crow_ref, o_ref, acc_scr = refs
    else:
        qt_ref, k_ref, vt_ref, o_ref, acc_scr = refs
    qi = pl.program_id(2)
    tiles_per_step = tk // t
    last_step = qi // tiles_per_step
    diag_offset = (qi % tiles_per_step) * t

    qt = qt_ref[0]
    feat = lax.broadcasted_iota(jnp.int32, qt.shape, 0)
    zero = jnp.zeros_like(qt)
    q2t = jnp.concatenate([jnp.where(feat < HEAD_DIM, qt, zero),
                           jnp.where(feat >= HEAD_DIM, qt, zero)], axis=1)
    if mode == "fox":
        cq = jnp.concatenate([crow_ref[0, 0, 0:1, :], crow_ref[0, 0, 1:2, :]], axis=1)
        hi, mid, lo = (p.astype(F32) for p in _split3(cq))
        r = lax.broadcasted_iota(jnp.int32, (LANES, 2 * t), 0)
        head_a = lax.broadcasted_iota(jnp.int32, (LANES, 2 * t), 1) < t
        qex = jnp.where(r < N_SPLIT, jnp.where(head_a, 1.0, 0.0),
              jnp.where(r < 2 * N_SPLIT, jnp.where(head_a, 0.0, 1.0),
              jnp.where(r == 2 * N_SPLIT, hi,
              jnp.where(r == 2 * N_SPLIT + 1, mid,
              jnp.where(r == 2 * N_SPLIT + 2, lo, 0.0)))))
        q2t = jnp.concatenate([q2t, qex.astype(BF16)], axis=0)

    acc_scr[...] = jnp.zeros_like(acc_scr)

    def causal(strict):
        key = lax.broadcasted_iota(jnp.int32, (tk, 2 * t), 0)
        col = lax.broadcasted_iota(jnp.int32, (tk, 2 * t), 1)
        query = jnp.where(col >= t, col - t, col) + diag_offset
        return key < query if strict else key <= query

    def scores(step):
        start = pl.multiple_of(step * tk, tk)
        k = k_ref[0, pl.ds(start, tk), :]
        if mode == "fox":
            k = jnp.concatenate([k, ex_ref[0, pl.ds(start, tk), :]], axis=1)
        return _dot(k, q2t), vt_ref[0, :, pl.ds(start, tk)]

    def pair_output(ot):
        f = lax.broadcasted_iota(jnp.int32, (LANES, t), 0)
        return jnp.where(f < HEAD_DIM, ot[:, :t], ot[:, t:]).T

    if mode == "sb":
        later_key = jnp.where(
            lax.broadcasted_iota(jnp.int32, (t, t), 1) > lax.broadcasted_iota(jnp.int32, (t, t), 0),
            1.0, 0.0).astype(BF16)

        def sb_step(step, rest_sum, masked):
            z, vt = scores(step)
            sp = _softplus(z)
            log_beta = z - sp
            log_rest = -sp
            if masked:
                valid = causal(strict=True)
                log_rest = jnp.where(valid, log_rest, 0.0)
            hi = log_rest.astype(BF16)
            lo = (log_rest - hi.astype(F32)).astype(BF16)
            for u in reversed(range(tiles_per_step)):
                rows = slice(u * t, (u + 1) * t)
                suffix = _dot(later_key, hi[rows]) + _dot(later_key, lo[rows]) + rest_sum
                a = jnp.exp(log_beta[rows] + suffix)
                if masked:
                    a = jnp.where(valid[rows], a, 0.0)
                acc_scr[...] += _dot(vt[:, rows], a.astype(BF16))
                rest_sum = rest_sum + jnp.sum(log_rest[rows], axis=0, keepdims=True)
            return rest_sum

        rest_sum = sb_step(last_step, jnp.zeros((1, 2 * t), F32), True)
        lax.fori_loop(0, last_step, lambda n, rs: sb_step(last_step - 1 - n, rs, False), rest_sum)
        o_ref[0] = pair_output(acc_scr[...]).astype(o_ref.dtype)
        return

    def softmax_step(step, carry, masked):
        m_old, l_old = carry
        s, vt = scores(step)
        if masked:
            s = jnp.where(causal(strict=False), s, -jnp.inf)
        m_new = jnp.maximum(m_old, jnp.max(s, axis=0, keepdims=True))
        alpha = jnp.exp(m_old - m_new)
        p = jnp.exp(s - m_new)
        acc_scr[...] = alpha * acc_scr[...] + _dot(vt, p.astype(BF16))
        return m_new, alpha * l_old + jnp.sum(p, axis=0, keepdims=True)

    init = (jnp.full((1, 2 * t), -jnp.inf, F32), jnp.zeros((1, 2 * t), F32))
    carry = lax.fori_loop(0, last_step, lambda n, c: softmax_step(n, c, False), init)
    _, l = softmax_step(last_step, carry, True)

    ot = acc_scr[...] / l
    if mode == "fox":
        o_ref[0] = pair_output(ot).astype(o_ref.dtype)
    else:
        lam = lam_ref[...]
        lam_full = (jnp.exp(jnp.sum(lam[0:1] * lam[1:2], axis=-1, keepdims=True))
                    - jnp.exp(jnp.sum(lam[2:3] * lam[3:4], axis=-1, keepdims=True))
                    + lambda_init)
        d = (ot[:, :t] - lam_full * ot[:, t:]).T
        o_ref[0] = (_rms(d, subln_ref[...]) * (1.0 - lambda_init)).astype(o_ref.dtype)


def _attention(qt, k, vt, mode, *, extra=(), lambda_init=0.0):
    bsz, d, seq = qt.shape
    t = min(ATTN_Q_TILE, seq)
    tk = min(ATTN_KV_STEP, seq)
    assert seq % tk == 0 and tk % t == 0
    in_specs = [
        pl.BlockSpec((1, LANES, t), lambda b, h, i: (b, h, i)),
        pl.BlockSpec((1, seq, LANES), lambda b, h, i: (b, 0, h)),
        pl.BlockSpec((1, LANES, seq), lambda b, h, i: (b, h, 0)),
    ]
    if mode == "diff":
        in_specs += [pl.BlockSpec((4, HEAD_DIM), lambda b, h, i: (0, 0)),
                     pl.BlockSpec((1, LANES), lambda b, h, i: (0, 0))]
    elif mode == "fox":
        in_specs += [pl.BlockSpec((1, seq, LANES), lambda b, h, i: (b, 0, h)),
                     pl.BlockSpec((1, 1, 2, t), lambda b, h, i: (b, h, 0, i))]
    return pl.pallas_call(
        functools.partial(_attn_kernel, mode=mode, t=t, tk=tk, lambda_init=lambda_init),
        grid=(bsz, d // LANES, seq // t),
        in_specs=in_specs,
        out_specs=pl.BlockSpec((1, t, LANES), lambda b, h, i: (b, i, h)),
        out_shape=jax.ShapeDtypeStruct((bsz, seq, d), BF16),
        scratch_shapes=[pltpu.VMEM((LANES, 2 * t), F32)],
        compiler_params=_params("parallel", "parallel", "arbitrary"),
        name="attn_" + mode,
    )(qt, k, vt, *extra)


def _outproj_xattn_kernel(x_ref, o_ref, wo_ref, g_ref, wq_ref, kv_ref, wxo_ref, out_ref):
    d = x_ref.shape[1]
    dh = d // XATTN_HEADS
    x1 = x_ref[...] + _dot(o_ref[...], wo_ref[...])
    h = _rms(x1, g_ref[...]).astype(BF16)
    q = (_dot(h, wq_ref[...]) * (dh ** -0.5)).astype(BF16)
    heads = []
    for hd in range(XATTN_HEADS):
        k = kv_ref[0, :, hd * dh:(hd + 1) * dh]
        v = kv_ref[0, :, d + hd * dh:d + (hd + 1) * dh]
        s = _dot_nt(q[:, hd * dh:(hd + 1) * dh], k)
        p = jnp.exp(s - jnp.max(s, axis=-1, keepdims=True))
        inv = 1.0 / jnp.sum(p, axis=-1, keepdims=True)
        heads.append((_dot(p.astype(BF16), v) * inv).astype(BF16))
    out_ref[...] = x1 + _dot(jnp.concatenate(heads, axis=1), wxo_ref[...])


def _outproj_xattn(x, o, w_o, gain, w_q, kv, w_xo, *, seq):
    m, d = x.shape
    n_mem = kv.shape[1]
    tm = min(ROW_TILE // 2, seq)
    tiles_per_seq = seq // tm
    full = lambda i: (0, 0)
    return pl.pallas_call(
        _outproj_xattn_kernel,
        grid=(m // tm,),
        in_specs=[
            pl.BlockSpec((tm, d), lambda i: (i, 0)),
            pl.BlockSpec((tm, d), lambda i: (i, 0)),
            pl.BlockSpec((d, d), full),
            pl.BlockSpec((1, d), full),
            pl.BlockSpec((d, d), full),
            pl.BlockSpec((1, n_mem, 2 * d), lambda i: (i // tiles_per_seq, 0, 0)),
            pl.BlockSpec((d, d), full),
        ],
        out_specs=pl.BlockSpec((tm, d), lambda i: (i, 0)),
        out_shape=jax.ShapeDtypeStruct((m, d), F32),
        compiler_params=_params("parallel"),
        name="outproj_xattn",
    )(x, o, w_o, gain.reshape(1, d), w_q, kv, w_xo)


def _rope_tables(seq):
    inv = 1.0 / (ROPE_THETA ** (jnp.arange(0, HEAD_DIM, 2, dtype=F32) / HEAD_DIM))
    ang = jnp.arange(seq, dtype=F32)[:, None] * inv[None, :]
    cos, sin = jnp.cos(ang), jnp.sin(ang)
    reps = LANES // HEAD_DIM
    return (cos.T, sin.T,
            jnp.tile(jnp.concatenate([cos, cos], axis=1), (1, reps)),
            jnp.tile(jnp.concatenate([-sin, sin], axis=1), (1, reps)))


def kernel(x, mem, ffn_norm, ffn_w_in, ffn_w_out, mix_norm, diff_w_qkv, diff_lambda, diff_subln, diff_w_o, sb_w_qkv, sb_w_o, fox_w_qkvf, fox_b_f, fox_w_o, xattn_norm, mem_norm, xattn_w_q, xattn_w_kv, xattn_w_o, final_norm):
    bsz, seq, d = x.shape
    n_mem = mem.shape[1]
    depth = ffn_norm.shape[0]
    n_heads = d // HEAD_DIM
    m = bsz * seq
    assert d % LANES == 0 and seq % min(ATTN_Q_TILE, seq) == 0 and seq % min(ROW_TILE, seq) == 0
    rope = _rope_tables(seq)
    qk_scale = HEAD_DIM ** -0.5
    bf = lambda w: w.astype(BF16)

    xf = x.reshape(m, d)
    memf = mem.reshape(bsz * n_mem, d)
    for i in range(depth):
        mixer, j = i % N_MIXERS, i // N_MIXERS
        xf = _ffn(xf, ffn_norm[i, 0], bf(ffn_w_in[i, 0]), bf(ffn_w_out[i, 0]))

        w_qkv = (diff_w_qkv, sb_w_qkv, fox_w_qkvf)[mixer][j]
        qt, k, vt = _qkv_proj(xf, mix_norm[i], w_qkv[:, :d], w_qkv[:, d:2 * d], w_qkv[:, 2 * d:3 * d],
                              bsz=bsz, seq=seq, scale=qk_scale, rope=rope if mixer == 0 else None)
        k = k.reshape(bsz, seq, d)
        if mixer == 0:
            lambda_init = 0.8 - 0.6 * math.exp(-0.3 * i)
            o = _attention(qt, k, vt, "diff", lambda_init=lambda_init,
                           extra=(diff_lambda[j], diff_subln[j].reshape(1, LANES)))
            w_o = diff_w_o[j]
        elif mixer == 1:
            o = _attention(qt, k, vt, "sb")
            w_o = sb_w_o[j]
        else:
            w_gate = jnp.pad(bf(w_qkv[:, 3 * d:]), ((0, 0), (0, LANES - n_heads)))
            z = _norm_matmul(xf, mix_norm[i], w_gate, out_dtype=F32, tn=LANES)
            b_pad = jnp.pad(fox_b_f[j], (0, LANES - n_heads)).reshape(1, LANES)
            c_row, k_bias = _fox_gate(z.reshape(bsz, seq, LANES), b_pad, n_heads, d)
            o = _attention(qt, k, vt, "fox",
                           extra=(k_bias, c_row.reshape(bsz, n_heads // 2, 2, seq)))
            w_o = fox_w_o[j]

        kv = _norm_matmul(memf, mem_norm, bf(xattn_w_kv[i]), out_dtype=BF16, tn=d)
        xf = _outproj_xattn(xf, o.reshape(m, d), bf(w_o), xattn_norm[i], bf(xattn_w_q[i]),
                            kv.reshape(bsz, n_mem, 2 * d), bf(xattn_w_o[i]), seq=seq)

        xf = _ffn(xf, ffn_norm[i, 1], bf(ffn_w_in[i, 1]), bf(ffn_w_out[i, 1]),
                  final_gain=final_norm if i == depth - 1 else None)
    return xf.reshape(bsz, seq, d)
```

```python
import functools
import math

import numpy as np
import jax
import jax.numpy as jnp
from jax import lax
from jax.experimental import pallas as pl
from jax.experimental.pallas import tpu as pltpu

F32 = jnp.float32
BF16 = jnp.bfloat16

EPS = 1e-5
ROPE_THETA = 10000.0
N_MIXERS = 3
HEAD_DIM = 64
XATTN_HEADS = 4
LANES = 128
VMEM_LIMIT_BYTES = 48 * 1024 * 1024

ROW_TILE = 1024
FFN_TILE = 256
ATTN_Q_TILE = 256
ATTN_KV_STEP = 1024
GATE_TILE = 256
N_SPLIT = 3
ONES_ROWS = 16
V_ROWS = LANES + ONES_ROWS
LOG2E = math.log2(math.e)


def _params(*semantics):
    return pltpu.CompilerParams(dimension_semantics=semantics,
                                vmem_limit_bytes=VMEM_LIMIT_BYTES)


def _rms(x, g):
    var = jnp.mean(x * x, axis=-1, keepdims=True)
    return x * lax.rsqrt(var + EPS) * g


def _dot(a, b):
    return jnp.dot(a, b, preferred_element_type=F32)


def _dot_nt(a, b):
    return lax.dot_general(a, b, (((1,), (1,)), ((), ())), preferred_element_type=F32)


def _softplus(z):
    return jnp.maximum(z, 0.0) + jnp.log(1.0 + jnp.exp(-jnp.abs(z)))


def _split3(x):
    hi = x.astype(BF16)
    r1 = x - hi.astype(F32)
    mid = r1.astype(BF16)
    lo = (r1 - mid.astype(F32)).astype(BF16)
    return hi, mid, lo


def _norm_matmul_kernel(x_ref, g_ref, w_ref, o_ref, h_scr):
    @pl.when(pl.program_id(1) == 0)
    def _():
        h_scr[...] = _rms(x_ref[...], g_ref[...]).astype(BF16)

    o_ref[...] = _dot(h_scr[...], w_ref[...]).astype(o_ref.dtype)


def _norm_matmul(x, gain, w, *, out_dtype, tn):
    m, d = x.shape
    n_cols = w.shape[1]
    tm = min(ROW_TILE, m)
    return pl.pallas_call(
        _norm_matmul_kernel,
        grid=(m // tm, n_cols // tn),
        in_specs=[pl.BlockSpec((tm, d), lambda i, n: (i, 0)),
                  pl.BlockSpec((1, d), lambda i, n: (0, 0)),
                  pl.BlockSpec((d, tn), lambda i, n: (0, n))],
        out_specs=pl.BlockSpec((tm, tn), lambda i, n: (i, n)),
        out_shape=jax.ShapeDtypeStruct((m, n_cols), out_dtype),
        scratch_shapes=[pltpu.VMEM((tm, d), BF16)],
        compiler_params=_params("parallel", "arbitrary"),
        name="norm_matmul",
    )(x, gain.reshape(1, d), w)


def _rope_swap(y):
    lane = lax.broadcasted_iota(jnp.int32, y.shape, 1)
    first_half = (lane % HEAD_DIM) < (HEAD_DIM // 2)
    return jnp.where(first_half, pltpu.roll(y, LANES - HEAD_DIM // 2, axis=1),
                     pltpu.roll(y, HEAD_DIM // 2, axis=1))


def _qkv_kernel(x_ref, g_ref, wqt_ref, wk_ref, wvt_ref, *rest, rope, scale):
    if rope:
        cost_ref, sint_ref, cos_ref, sin_ref, qt_ref, k_ref, vt_ref = rest
    else:
        qt_ref, k_ref, vt_ref = rest
    h = _rms(x_ref[...], g_ref[...]).astype(BF16)
    qt = _dot_nt(wqt_ref[...], h) * scale
    k = _dot(h, wk_ref[...])
    if rope:
        half = HEAD_DIM // 2
        ct, st = cost_ref[...], sint_ref[...]
        for r0 in range(0, qt.shape[0], HEAD_DIM):
            x1, x2 = qt[r0:r0 + half], qt[r0 + half:r0 + HEAD_DIM]
            qt_ref[0, r0:r0 + half, :] = (x1 * ct - x2 * st).astype(BF16)
            qt_ref[0, r0 + half:r0 + HEAD_DIM, :] = (x2 * ct + x1 * st).astype(BF16)
        cos, sin = cos_ref[...], sin_ref[...]
        for c0 in range(0, k.shape[1], LANES):
            kc = k[:, c0:c0 + LANES]
            k_ref[:, c0:c0 + LANES] = (kc * cos + _rope_swap(kc) * sin).astype(BF16)
    else:
        qt_ref[0] = qt.astype(BF16)
        k_ref[...] = k.astype(BF16)
    vt = _dot_nt(wvt_ref[...], h)
    for pair in range(vt_ref.shape[1]):
        vt_ref[0, pair, :LANES, :] = vt[pair * LANES:(pair + 1) * LANES].astype(BF16)
        vt_ref[0, pair, LANES:, :] = jnp.ones((ONES_ROWS, vt.shape[1]), BF16)


def _qkv_proj(x, gain, w_q, w_k, w_v, *, bsz, seq, scale, rope=None):
    m, d = x.shape
    tm = min(ROW_TILE // 2, seq)
    tiles_per_seq = seq // tm
    n_pairs = d // LANES
    full = lambda i: (0, 0)
    feat_major = pl.BlockSpec((1, d, tm), lambda i: (i // tiles_per_seq, 0, i % tiles_per_seq))
    values_spec = pl.BlockSpec((1, n_pairs, V_ROWS, tm),
                               lambda i: (i // tiles_per_seq, 0, 0, i % tiles_per_seq))
    in_specs = [pl.BlockSpec((tm, d), lambda i: (i, 0)),
                pl.BlockSpec((1, d), full),
                pl.BlockSpec((d, d), full), pl.BlockSpec((d, d), full), pl.BlockSpec((d, d), full)]
    args = [x, gain.reshape(1, d), w_q.T.astype(BF16), w_k.astype(BF16), w_v.T.astype(BF16)]
    if rope is not None:
        half = HEAD_DIM // 2
        in_specs += [pl.BlockSpec((half, tm), lambda i: (0, i % tiles_per_seq))] * 2
        in_specs += [pl.BlockSpec((tm, LANES), lambda i: (i % tiles_per_seq, 0))] * 2
        args += list(rope)
    return pl.pallas_call(
        functools.partial(_qkv_kernel, rope=rope is not None, scale=scale),
        grid=(m // tm,),
        in_specs=in_specs,
        out_specs=[feat_major, pl.BlockSpec((tm, d), lambda i: (i, 0)), values_spec],
        out_shape=[jax.ShapeDtypeStruct((bsz, d, seq), BF16),
                   jax.ShapeDtypeStruct((m, d), BF16),
                   jax.ShapeDtypeStruct((bsz, n_pairs, V_ROWS, seq), BF16)],
        compiler_params=_params("parallel"),
        name="qkv_proj",
    )(*args)


def _ffn_kernel(x_ref, g_ref, win_ref, wo_ref, *rest, final_norm):
    if final_norm:
        fg_ref, o_ref, act_scr = rest
    else:
        o_ref, act_scr = rest
    d_ff = wo_ref.shape[0]
    x = x_ref[...]
    h = _rms(x, g_ref[...]).astype(BF16)
    for c0 in range(0, d_ff, FFN_TILE):
        gate = _dot(h, win_ref[:, c0:c0 + FFN_TILE])
        up = _dot(h, win_ref[:, d_ff + c0:d_ff + c0 + FFN_TILE])
        act_scr[:, c0:c0 + FFN_TILE] = (gate * jax.nn.sigmoid(gate) * up).astype(BF16)
    y = x + 0.5 * _dot(act_scr[...], wo_ref[...])
    if final_norm:
        y = _rms(y, fg_ref[...])
    o_ref[...] = y


def _ffn(x, gain, w_in, w_out, final_gain=None):
    m, d = x.shape
    d_ff = w_out.shape[0]
    tm = min(ROW_TILE // 2, m)
    full = lambda i: (0, 0)
    resident = pl.Buffered(1)
    in_specs = [
        pl.BlockSpec((tm, d), lambda i: (i, 0)),
        pl.BlockSpec((1, d), full),
        pl.BlockSpec((d, 2 * d_ff), full, pipeline_mode=resident),
        pl.BlockSpec((d_ff, d), full, pipeline_mode=resident),
    ]
    args = [x, gain.reshape(1, d), w_in, w_out]
    if final_gain is not None:
        in_specs.append(pl.BlockSpec((1, d), full))
        args.append(final_gain.reshape(1, d))
    return pl.pallas_call(
        functools.partial(_ffn_kernel, final_norm=final_gain is not None),
        grid=(m // tm,),
        in_specs=in_specs,
        out_specs=pl.BlockSpec((tm, d), lambda i: (i, 0)),
        out_shape=jax.ShapeDtypeStruct((m, d), F32),
        scratch_shapes=[pltpu.VMEM((tm, d_ff), BF16)],
        compiler_params=_params("parallel"),
        name="ffn",
    )(*args)


def _fox_placement(n_heads, d):
    place = np.zeros((N_SPLIT, LANES, d), np.float32)
    ones = np.zeros((1, d), np.float32)
    for h in range(n_heads):
        base = (h // 2) * LANES + (h % 2) * N_SPLIT
        for piece in range(N_SPLIT):
            place[piece, h, base + piece] = -1.0
    for pair in range(d // LANES):
        ones[0, pair * LANES + 2 * N_SPLIT:pair * LANES + 3 * N_SPLIT] = 1.0
    return jnp.asarray(place, BF16), jnp.asarray(ones, F32)


def _fox_gate_kernel(z_ref, b_ref, place_ref, ones_ref, crow_ref, ex_ref, carry_scr, *, n_heads):
    @pl.when(pl.program_id(1) == 0)
    def _():
        carry_scr[...] = jnp.zeros_like(carry_scr)

    z = z_ref[0] + b_ref[...]
    log_f = -_softplus(-z) * LOG2E
    ts = log_f.shape[0]
    row = lax.broadcasted_iota(jnp.int32, (ts, ts), 0)
    col = lax.broadcasted_iota(jnp.int32, (ts, ts), 1)
    tri = jnp.where(row >= col, 1.0, 0.0).astype(BF16)
    hi, mid, lo = _split3(log_f)
    c = _dot(tri, hi) + _dot(tri, mid) + _dot(tri, lo) + carry_scr[...]
    carry_scr[...] = c[ts - 1:ts, :]
    crow_ref[0] = c.T[:n_heads]
    hi, mid, lo = _split3(c)
    ex = _dot(hi, place_ref[0]) + _dot(mid, place_ref[1]) + _dot(lo, place_ref[2]) + ones_ref[...]
    ex_ref[0] = ex.astype(BF16)


def _fox_gate(z, b_pad, n_heads, d):
    bsz, seq, _ = z.shape
    ts = min(GATE_TILE, seq)
    place, ones = _fox_placement(n_heads, d)
    return pl.pallas_call(
        functools.partial(_fox_gate_kernel, n_heads=n_heads),
        grid=(bsz, seq // ts),
        in_specs=[pl.BlockSpec((1, ts, LANES), lambda b, t: (b, t, 0)),
                  pl.BlockSpec((1, LANES), lambda b, t: (0, 0)),
                  pl.BlockSpec((N_SPLIT, LANES, d), lambda b, t: (0, 0, 0)),
                  pl.BlockSpec((1, d), lambda b, t: (0, 0))],
        out_specs=[pl.BlockSpec((1, n_heads, ts), lambda b, t: (b, 0, t)),
                   pl.BlockSpec((1, ts, d), lambda b, t: (b, t, 0))],
        out_shape=[jax.ShapeDtypeStruct((bsz, n_heads, seq), F32),
                   jax.ShapeDtypeStruct((bsz, seq, d), BF16)],
        scratch_shapes=[pltpu.VMEM((1, LANES), F32)],
        compiler_params=_params("parallel", "arbitrary"),
        name="fox_gate",
    )(z, b_pad, place, ones)


def _attn_kernel(*refs, mode, t, tk, lambda_init):
    if mode == "diff":
        qt_ref, k_ref, vt_ref, lam_ref, subln_ref, o_ref, acc_scr, p_scr, s_scr = refs
    elif mode == "fox":
        qt_ref, k_ref, vt_ref, ex_ref, crow_ref, o_ref, acc_scr, p_scr, s_scr = refs
    else:
        qt_ref, k_ref, vt_ref, o_ref, acc_scr, p_scr, lb_scr, hl_scr = refs
    qi = pl.program_id(2)
    tiles_per_step = tk // t
    subs = range(tiles_per_step)
    last_step = qi // tiles_per_step
    diag_offset = (qi % tiles_per_step) * t

    qt = qt_ref[0]
    feat = lax.broadcasted_iota(jnp.int32, qt.shape, 0)
    zero = jnp.zeros_like(qt)
    q2t = jnp.concatenate([jnp.where(feat < HEAD_DIM, qt, zero),
                           jnp.where(feat >= HEAD_DIM, qt, zero)], axis=1)
    if mode == "fox":
        cq = jnp.concatenate([crow_ref[0, 0, 0:1, :], crow_ref[0, 0, 1:2, :]], axis=1)
        hi, mid, lo = (p.astype(F32) for p in _split3(cq))
        r = lax.broadcasted_iota(jnp.int32, (LANES, 2 * t), 0)
        head_a = lax.broadcasted_iota(jnp.int32, (LANES, 2 * t), 1) < t
        qex = jnp.where(r < N_SPLIT, jnp.where(head_a, 1.0, 0.0),
              jnp.where(r < 2 * N_SPLIT, jnp.where(head_a, 0.0, 1.0),
              jnp.where(r == 2 * N_SPLIT, hi,
              jnp.where(r == 2 * N_SPLIT + 1, mid,
              jnp.where(r == 2 * N_SPLIT + 2, lo, 0.0)))))
        q2t = jnp.concatenate([q2t, qex.astype(BF16)], axis=0)

    acc_scr[...] = jnp.zeros_like(acc_scr)
    p_scr[...] = jnp.zeros_like(p_scr)

    def sub(u):
        return pl.ds(u * t, t)

    def causal(u, strict):
        key = lax.broadcasted_iota(jnp.int32, (t, 2 * t), 0) + u * t
        col = lax.broadcasted_iota(jnp.int32, (t, 2 * t), 1)
        query = jnp.where(col >= t, col - t, col) + diag_offset
        return key < query if strict else key <= query

    def scores(step, u):
        start = pl.multiple_of(step * tk + u * t, t)
        k = k_ref[0, pl.ds(start, t), :]
        if mode == "fox":
            k = jnp.concatenate([k, ex_ref[0, pl.ds(start, t), :]], axis=1)
        return _dot(k, q2t)

    def values(step, rows=V_ROWS):
        return vt_ref[0, 0, :rows, pl.ds(pl.multiple_of(step * tk, tk), tk)]

    def pair_output(ot):
        f = lax.broadcasted_iota(jnp.int32, (LANES, t), 0)
        return jnp.where(f < HEAD_DIM, ot[:, :t], ot[:, t:]).T

    if mode == "sb":
        later_key = (lax.broadcasted_iota(jnp.int32, (t + ONES_ROWS, t), 1)
                     > lax.broadcasted_iota(jnp.int32, (t + ONES_ROWS, t), 0))
        all_keys = lax.broadcasted_iota(jnp.int32, (t + ONES_ROWS, t), 0) >= t
        later_sum = jnp.where(later_key | all_keys, 1.0, 0.0).astype(BF16)
        later_sum = jnp.concatenate([later_sum, later_sum], axis=1)

        def hilo(u):
            return pl.ds(2 * u * t, 2 * t)

        def produce(step, u, masked):
            z = scores(step, u)
            neg_abs = lax.bitcast_convert_type(
                lax.bitcast_convert_type(z, jnp.uint32) | jnp.uint32(0x80000000), F32)
            log_beta = jnp.minimum(z, 0.0) - jnp.log(1.0 + jnp.exp2(neg_abs)) * LOG2E
            log_rest = log_beta - z
            if masked:
                valid = causal(u, strict=True)
                log_rest = jnp.where(valid, log_rest, 0.0)
                log_beta = jnp.where(valid, log_beta, -jnp.inf)
            hi = log_rest.astype(BF16)
            lo = (log_rest - hi.astype(F32)).astype(BF16)
            lb_scr[sub(u), :] = log_beta
            hl_scr[hilo(u), :] = jnp.concatenate([hi, lo], axis=0)

        def consume(u, rest_sum):
            sums = _dot(later_sum, hl_scr[hilo(u), :])
            p_scr[sub(u), :] = jnp.exp2(lb_scr[sub(u), :] + (sums[:t] + rest_sum)).astype(BF16)
            return rest_sum + sums[t:t + 1]

        def update(rest_sum, next_step):
            for u in reversed(subs):
                rest_sum = consume(u, rest_sum)
                if next_step is not None:
                    produce(next_step, u, False)
            return rest_sum

        def body(i, rest_sum):
            prev = last_step - jnp.maximum(i - 1, 0)
            acc_scr[...] += _dot(values(prev, LANES), p_scr[...])
            return update(rest_sum, last_step - 1 - i)

        for u in subs:
            produce(last_step, u, True)
        rest_sum = lax.fori_loop(0, last_step, body, jnp.zeros((1, 2 * t), F32))
        prev = last_step - jnp.maximum(last_step - 1, 0)
        pv_prev = _dot(values(prev, LANES), p_scr[...])
        update(rest_sum, None)
        ot = acc_scr[...] + pv_prev + _dot(values(0, LANES), p_scr[...])
        o_ref[0] = pair_output(ot).astype(o_ref.dtype)
        return

    def produce(step, u, masked):
        s = scores(step, u)
        if masked:
            s = jnp.where(causal(u, strict=False), s, -jnp.inf)
        s_scr[sub(u), :] = s
        return jnp.max(s, axis=0, keepdims=True)

    def consume(u, m_new):
        p_scr[sub(u), :] = jnp.exp2(s_scr[sub(u), :] - m_new).astype(BF16)

    def step_before(i):
        return jnp.where(i == 1, last_step, jnp.maximum(i - 2, 0))

    def body(i, carry):
        m_old, m_step, alpha_prev = carry
        acc_scr[...] = alpha_prev * acc_scr[...] + _dot(values(step_before(i)), p_scr[...])
        m_new = jnp.maximum(m_old, m_step)
        m_next = None
        for u in subs:
            consume(u, m_new)
            col_max = produce(i, u, False)
            m_next = col_max if m_next is None else jnp.maximum(m_next, col_max)
        return m_new, m_next, jnp.exp2(m_old - m_new)

    m_step = None
    for u in subs:
        col_max = produce(last_step, u, True)
        m_step = col_max if m_step is None else jnp.maximum(m_step, col_max)
    init = (jnp.full((1, 2 * t), -jnp.inf, F32), m_step, jnp.ones((1, 2 * t), F32))
    m_old, m_step, alpha_prev = lax.fori_loop(0, last_step, body, init)
    pv_prev = _dot(values(step_before(last_step)), p_scr[...])
    m_new = jnp.maximum(m_old, m_step)
    p = jnp.exp2(s_scr[...] - m_new).astype(BF16)
    pv = _dot(values(jnp.maximum(last_step - 1, 0)), p)
    acc = jnp.exp2(m_old - m_new) * (alpha_prev * acc_scr[...] + pv_prev) + pv
    ot = acc[:LANES] / acc[LANES:LANES + 1]
    if mode == "fox":
        o_ref[0] = pair_output(ot).astype(o_ref.dtype)
    else:
        lam = lam_ref[...]
        lam_full = (jnp.exp(jnp.sum(lam[0:1] * lam[1:2], axis=-1, keepdims=True))
                    - jnp.exp(jnp.sum(lam[2:3] * lam[3:4], axis=-1, keepdims=True))
                    + lambda_init)
        d = (ot[:, :t] - lam_full * ot[:, t:]).T
        o_ref[0] = (_rms(d, subln_ref[...]) * (1.0 - lambda_init)).astype(o_ref.dtype)


def _attention(qt, k, vt, mode, *, extra=(), lambda_init=0.0):
    bsz, d, seq = qt.shape
    t = min(ATTN_Q_TILE, seq)
    tk = min(ATTN_KV_STEP, seq)
    assert seq % tk == 0 and tk % t == 0
    in_specs = [
        pl.BlockSpec((1, LANES, t), lambda b, h, i: (b, h, i)),
        pl.BlockSpec((1, seq, LANES), lambda b, h, i: (b, 0, h)),
        pl.BlockSpec((1, 1, V_ROWS, seq), lambda b, h, i: (b, h, 0, 0)),
    ]
    if mode == "sb":
        scratch = [pltpu.VMEM((LANES, 2 * t), F32), pltpu.VMEM((tk, 2 * t), BF16),
                   pltpu.VMEM((tk, 2 * t), F32), pltpu.VMEM((2 * tk, 2 * t), BF16)]
    else:
        scratch = [pltpu.VMEM((V_ROWS, 2 * t), F32), pltpu.VMEM((tk, 2 * t), BF16),
                   pltpu.VMEM((tk, 2 * t), F32)]
    if mode == "diff":
        in_specs += [pl.BlockSpec((4, HEAD_DIM), lambda b, h, i: (0, 0)),
                     pl.BlockSpec((1, LANES), lambda b, h, i: (0, 0))]
    elif mode == "fox":
        in_specs += [pl.BlockSpec((1, seq, LANES), lambda b, h, i: (b, 0, h)),
                     pl.BlockSpec((1, 1, 2, t), lambda b, h, i: (b, h, 0, i))]
    return pl.pallas_call(
        functools.partial(_attn_kernel, mode=mode, t=t, tk=tk, lambda_init=lambda_init),
        grid=(bsz, d // LANES, seq // t),
        in_specs=in_specs,
        out_specs=pl.BlockSpec((1, t, LANES), lambda b, h, i: (b, i, h)),
        out_shape=jax.ShapeDtypeStruct((bsz, seq, d), BF16),
        scratch_shapes=scratch,
        compiler_params=_params("parallel", "parallel", "arbitrary"),
        name="attn_" + mode,
    )(qt, k, vt, *extra)


def _outproj_xattn_kernel(x_ref, o_ref, wo_ref, g_ref, wq_ref, kv_ref, wxo_ref, out_ref):
    d = x_ref.shape[1]
    dh = d // XATTN_HEADS
    x1 = x_ref[...] + _dot(o_ref[...], wo_ref[...])
    h = _rms(x1, g_ref[...]).astype(BF16)
    q = (_dot(h, wq_ref[...]) * (dh ** -0.5)).astype(BF16)
    heads = []
    for hd in range(XATTN_HEADS):
        k = kv_ref[0, :, hd * dh:(hd + 1) * dh]
        v = kv_ref[0, :, d + hd * dh:d + (hd + 1) * dh]
        s = _dot_nt(q[:, hd * dh:(hd + 1) * dh], k)
        p = jnp.exp(s - jnp.max(s, axis=-1, keepdims=True))
        inv = 1.0 / jnp.sum(p, axis=-1, keepdims=True)
        heads.append((_dot(p.astype(BF16), v) * inv).astype(BF16))
    out_ref[...] = x1 + _dot(jnp.concatenate(heads, axis=1), wxo_ref[...])


def _outproj_xattn(x, o, w_o, gain, w_q, kv, w_xo, *, seq):
    m, d = x.shape
    n_mem = kv.shape[1]
    tm = min(ROW_TILE // 2, seq)
    tiles_per_seq = seq // tm
    full = lambda i: (0, 0)
    return pl.pallas_call(
        _outproj_xattn_kernel,
        grid=(m // tm,),
        in_specs=[
            pl.BlockSpec((tm, d), lambda i: (i, 0)),
            pl.BlockSpec((tm, d), lambda i: (i, 0)),
            pl.BlockSpec((d, d), full),
            pl.BlockSpec((1, d), full),
            pl.BlockSpec((d, d), full),
            pl.BlockSpec((1, n_mem, 2 * d), lambda i: (i // tiles_per_seq, 0, 0)),
            pl.BlockSpec((d, d), full),
        ],
        out_specs=pl.BlockSpec((tm, d), lambda i: (i, 0)),
        out_shape=jax.ShapeDtypeStruct((m, d), F32),
        compiler_params=_params("parallel"),
        name="outproj_xattn",
    )(x, o, w_o, gain.reshape(1, d), w_q, kv, w_xo)


def _rope_tables(seq):
    inv = 1.0 / (ROPE_THETA ** (jnp.arange(0, HEAD_DIM, 2, dtype=F32) / HEAD_DIM))
    ang = jnp.arange(seq, dtype=F32)[:, None] * inv[None, :]
    cos, sin = jnp.cos(ang), jnp.sin(ang)
    reps = LANES // HEAD_DIM
    return (cos.T, sin.T,
            jnp.tile(jnp.concatenate([cos, cos], axis=1), (1, reps)),
            jnp.tile(jnp.concatenate([-sin, sin], axis=1), (1, reps)))


def kernel(x, mem, ffn_norm, ffn_w_in, ffn_w_out, mix_norm, diff_w_qkv, diff_lambda, diff_subln, diff_w_o, sb_w_qkv, sb_w_o, fox_w_qkvf, fox_b_f, fox_w_o, xattn_norm, mem_norm, xattn_w_q, xattn_w_kv, xattn_w_o, final_norm):
    bsz, seq, d = x.shape
    n_mem = mem.shape[1]
    depth = ffn_norm.shape[0]
    n_heads = d // HEAD_DIM
    m = bsz * seq
    assert d % LANES == 0 and seq % min(ATTN_Q_TILE, seq) == 0 and seq % min(ROW_TILE, seq) == 0
    rope = _rope_tables(seq)
    qk_scale = HEAD_DIM ** -0.5 * LOG2E
    bf =lambda w: w.astype(BF16)

    xf = x.reshape(m, d)
    memf = mem.reshape(bsz * n_mem, d)
    for i in range(depth):
        mixer, j = i % N_MIXERS, i // N_MIXERS
        xf = _ffn(xf, ffn_norm[i, 0], bf(ffn_w_in[i, 0]), bf(ffn_w_out[i, 0]))

        w_qkv = (diff_w_qkv, sb_w_qkv, fox_w_qkvf)[mixer][j]
        qt, k, vt = _qkv_proj(xf, mix_norm[i], w_qkv[:, :d], w_qkv[:, d:2 * d], w_qkv[:, 2 * d:3 * d],
                              bsz=bsz, seq=seq, scale=qk_scale, rope=rope if mixer == 0 else None)
        k = k.reshape(bsz, seq, d)
        if mixer == 0:
            lambda_init = 0.8 - 0.6 * math.exp(-0.3 * i)
            o = _attention(qt, k, vt, "diff", lambda_init=lambda_init,
                           extra=(diff_lambda[j], diff_subln[j].reshape(1, LANES)))
            w_o = diff_w_o[j]
        elif mixer == 1:
            o = _attention(qt, k, vt, "sb")
            w_o = sb_w_o[j]
        else:
            w_gate = jnp.pad(bf(w_qkv[:, 3 * d:]), ((0, 0), (0, LANES - n_heads)))
            z = _norm_matmul(xf, mix_norm[i], w_gate, out_dtype=F32, tn=LANES)
            b_pad = jnp.pad(fox_b_f[j], (0, LANES - n_heads)).reshape(1, LANES)
            c_row, k_bias = _fox_gate(z.reshape(bsz, seq, LANES), b_pad, n_heads, d)
            o = _attention(qt, k, vt, "fox",
                           extra=(k_bias, c_row.reshape(bsz, n_heads // 2, 2, seq)))
            w_o = fox_w_o[j]

        kv = _norm_matmul(memf, mem_norm, bf(xattn_w_kv[i]), out_dtype=BF16, tn=d)
        xf = _outproj_xattn(xf, o.reshape(m, d), bf(w_o), xattn_norm[i], bf(xattn_w_q[i]),
                            kv.reshape(bsz, n_mem, 2 * d), bf(xattn_w_o[i]), seq=seq)

        xf = _ffn(xf, ffn_norm[i, 1], bf(ffn_w_in[i, 1]), bf(ffn_w_out[i, 1]),
                  final_gain=final_norm if i == depth - 1 else None)
    return xf.reshape(bsz, seq, d)
```

```python
import functools
import math

import numpy as np
import jax
import jax.numpy as jnp
from jax import lax
from jax.experimental import pallas as pl
from jax.experimental.pallas import tpu as pltpu

F32 = jnp.float32
BF16 = jnp.bfloat16

EPS = 1e-5
ROPE_THETA = 10000.0
N_MIXERS = 3
HEAD_DIM = 64
XATTN_HEADS = 4
LANES = 128
VMEM_LIMIT_BYTES = 48 * 1024 * 1024

ROW_TILE = 1024
FFN_TILE = 256
ATTN_Q_TILE = 256
ATTN_KV_STEP = 1024
GATE_TILE = 256
N_SPLIT = 3
ONES_ROWS = 16
SUM_TILE = 128
V_ROWS = LANES + ONES_ROWS
LOG2E = math.log2(math.e)


def _params(*semantics):
    return pltpu.CompilerParams(dimension_semantics=semantics,
                                vmem_limit_bytes=VMEM_LIMIT_BYTES)


def _rms(x, g):
    var = jnp.mean(x * x, axis=-1, keepdims=True)
    return x * lax.rsqrt(var + EPS) * g


def _dot(a, b):
    return jnp.dot(a, b, preferred_element_type=F32)


def _dot_nt(a, b):
    return lax.dot_general(a, b, (((1,), (1,)), ((), ())), preferred_element_type=F32)


def _softplus(z):
    return jnp.maximum(z, 0.0) + jnp.log(1.0 + jnp.exp(-jnp.abs(z)))


def _split3(x):
    hi = x.astype(BF16)
    r1 = x - hi.astype(F32)
    mid = r1.astype(BF16)
    lo = (r1 - mid.astype(F32)).astype(BF16)
    return hi, mid, lo


def _norm_matmul_kernel(x_ref, g_ref, w_ref, o_ref, h_scr):
    @pl.when(pl.program_id(1) == 0)
    def _():
        h_scr[...] = _rms(x_ref[...], g_ref[...]).astype(BF16)

    o_ref[...] = _dot(h_scr[...], w_ref[...]).astype(o_ref.dtype)


def _norm_matmul(x, gain, w, *, out_dtype, tn):
    m, d = x.shape
    n_cols = w.shape[1]
    tm = min(ROW_TILE, m)
    return pl.pallas_call(
        _norm_matmul_kernel,
        grid=(m // tm, n_cols // tn),
        in_specs=[pl.BlockSpec((tm, d), lambda i, n: (i, 0)),
                  pl.BlockSpec((1, d), lambda i, n: (0, 0)),
                  pl.BlockSpec((d, tn), lambda i, n: (0, n))],
        out_specs=pl.BlockSpec((tm, tn), lambda i, n: (i, n)),
        out_shape=jax.ShapeDtypeStruct((m, n_cols), out_dtype),
        scratch_shapes=[pltpu.VMEM((tm, d), BF16)],
        compiler_params=_params("parallel", "arbitrary"),
        name="norm_matmul",
    )(x, gain.reshape(1, d), w)


def _rope_swap(y):
    lane = lax.broadcasted_iota(jnp.int32, y.shape, 1)
    first_half = (lane % HEAD_DIM) < (HEAD_DIM // 2)
    return jnp.where(first_half, pltpu.roll(y, LANES - HEAD_DIM // 2, axis=1),
                     pltpu.roll(y, HEAD_DIM // 2, axis=1))


def _qkv_kernel(x_ref, g_ref, wqt_ref, wk_ref, wvt_ref, *rest, rope, scale):
    if rope:
        cost_ref, sint_ref, cos_ref, sin_ref, qt_ref, k_ref, vt_ref = rest
    else:
        qt_ref, k_ref, vt_ref = rest
    h = _rms(x_ref[...], g_ref[...]).astype(BF16)
    qt = _dot_nt(wqt_ref[...], h) * scale
    k = _dot(h, wk_ref[...])
    if rope:
        half = HEAD_DIM // 2
        ct, st = cost_ref[...], sint_ref[...]
        for r0 in range(0, qt.shape[0], HEAD_DIM):
            x1, x2 = qt[r0:r0 + half], qt[r0 + half:r0 + HEAD_DIM]
            qt_ref[0, r0:r0 + half, :] = (x1 * ct - x2 * st).astype(BF16)
            qt_ref[0, r0 + half:r0 + HEAD_DIM, :] = (x2 * ct + x1 * st).astype(BF16)
        cos, sin = cos_ref[...], sin_ref[...]
        for c0 in range(0, k.shape[1], LANES):
            kc = k[:, c0:c0 + LANES]
            k_ref[:, c0:c0 + LANES] = (kc * cos + _rope_swap(kc) * sin).astype(BF16)
    else:
        qt_ref[0] = qt.astype(BF16)
        k_ref[...] = k.astype(BF16)
    vt = _dot_nt(wvt_ref[...], h)
    for pair in range(vt_ref.shape[1]):
        vt_ref[0, pair, :LANES, :] = vt[pair * LANES:(pair + 1) * LANES].astype(BF16)
        vt_ref[0, pair, LANES:, :] = jnp.ones((ONES_ROWS, vt.shape[1]), BF16)


def _qkv_proj(x, gain, w_q, w_k, w_v, *, bsz, seq, scale, rope=None):
    m, d = x.shape
    tm = min(ROW_TILE // 2, seq)
    tiles_per_seq = seq // tm
    n_pairs = d // LANES
    full = lambda i: (0, 0)
    feat_major = pl.BlockSpec((1, d, tm), lambda i: (i // tiles_per_seq, 0, i % tiles_per_seq))
    values_spec = pl.BlockSpec((1, n_pairs, V_ROWS, tm),
                               lambda i: (i // tiles_per_seq, 0, 0, i % tiles_per_seq))
    in_specs = [pl.BlockSpec((tm, d), lambda i: (i, 0)),
                pl.BlockSpec((1, d), full),
                pl.BlockSpec((d, d), full), pl.BlockSpec((d, d), full), pl.BlockSpec((d, d), full)]
    args = [x, gain.reshape(1, d), w_q.T.astype(BF16), w_k.astype(BF16), w_v.T.astype(BF16)]
    if rope is not None:
        half = HEAD_DIM // 2
        in_specs += [pl.BlockSpec((half, tm), lambda i: (0, i % tiles_per_seq))] * 2
        in_specs += [pl.BlockSpec((tm, LANES), lambda i: (i % tiles_per_seq, 0))] * 2
        args += list(rope)
    return pl.pallas_call(
        functools.partial(_qkv_kernel, rope=rope is not None, scale=scale),
        grid=(m // tm,),
        in_specs=in_specs,
        out_specs=[feat_major, pl.BlockSpec((tm, d), lambda i: (i, 0)), values_spec],
        out_shape=[jax.ShapeDtypeStruct((bsz, d, seq), BF16),
                   jax.ShapeDtypeStruct((m, d), BF16),
                   jax.ShapeDtypeStruct((bsz, n_pairs, V_ROWS, seq), BF16)],
        compiler_params=_params("parallel"),
        name="qkv_proj",
    )(*args)


def _ffn_kernel(x_ref, g_ref, win_ref, wo_ref, *rest, final_norm):
    if final_norm:
        fg_ref, o_ref, act_scr = rest
    else:
        o_ref, act_scr = rest
    d_ff = wo_ref.shape[0]
    x = x_ref[...]
    h = _rms(x, g_ref[...]).astype(BF16)
    for c0 in range(0, d_ff, FFN_TILE):
        gate = _dot(h, win_ref[:, c0:c0 + FFN_TILE])
        up = _dot(h, win_ref[:, d_ff + c0:d_ff + c0 + FFN_TILE])
        act_scr[:, c0:c0 + FFN_TILE] = (gate * jax.nn.sigmoid(gate) * up).astype(BF16)
    y = x + 0.5 * _dot(act_scr[...], wo_ref[...])
    if final_norm:
        y = _rms(y, fg_ref[...])
    o_ref[...] = y


def _ffn(x, gain, w_in, w_out, final_gain=None):
    m, d = x.shape
    d_ff = w_out.shape[0]
    tm = min(ROW_TILE // 2, m)
    full = lambda i: (0, 0)
    resident = pl.Buffered(1)
    in_specs = [
        pl.BlockSpec((tm, d), lambda i: (i, 0)),
        pl.BlockSpec((1, d), full),
        pl.BlockSpec((d, 2 * d_ff), full, pipeline_mode=resident),
        pl.BlockSpec((d_ff, d), full, pipeline_mode=resident),
    ]
    args = [x, gain.reshape(1, d), w_in, w_out]
    if final_gain is not None:
        in_specs.append(pl.BlockSpec((1, d), full))
        args.append(final_gain.reshape(1, d))
    return pl.pallas_call(
        functools.partial(_ffn_kernel, final_norm=final_gain is not None),
        grid=(m // tm,),
        in_specs=in_specs,
        out_specs=pl.BlockSpec((tm, d), lambda i: (i, 0)),
        out_shape=jax.ShapeDtypeStruct((m, d), F32),
        scratch_shapes=[pltpu.VMEM((tm, d_ff), BF16)],
        compiler_params=_params("parallel"),
        name="ffn",
    )(*args)


def _fox_placement(n_heads, d):
    place = np.zeros((N_SPLIT, LANES, d), np.float32)
    ones = np.zeros((1, d), np.float32)
    for h in range(n_heads):
        base = (h // 2) * LANES + (h % 2) * N_SPLIT
        for piece in range(N_SPLIT):
            place[piece, h, base + piece] = -1.0
    for pair in range(d // LANES):
        ones[0, pair * LANES + 2 * N_SPLIT:pair * LANES + 3 * N_SPLIT] = 1.0
    return jnp.asarray(place, BF16), jnp.asarray(ones, F32)


def _fox_gate_kernel(z_ref, b_ref, place_ref, ones_ref, crow_ref, ex_ref, carry_scr, *, n_heads):
    @pl.when(pl.program_id(1) == 0)
    def _():
        carry_scr[...] = jnp.zeros_like(carry_scr)

    z = z_ref[0] + b_ref[...]
    log_f = -_softplus(-z) * LOG2E
    ts = log_f.shape[0]
    row = lax.broadcasted_iota(jnp.int32, (ts, ts), 0)
    col = lax.broadcasted_iota(jnp.int32, (ts, ts), 1)
    tri = jnp.where(row >= col, 1.0, 0.0).astype(BF16)
    hi, mid, lo = _split3(log_f)
    c = _dot(tri, hi) + _dot(tri, mid) + _dot(tri, lo) + carry_scr[...]
    carry_scr[...] = c[ts - 1:ts, :]
    crow_ref[0] = c.T[:n_heads]
    hi, mid, lo = _split3(c)
    ex = _dot(hi, place_ref[0]) + _dot(mid, place_ref[1]) + _dot(lo, place_ref[2]) + ones_ref[...]
    ex_ref[0] = ex.astype(BF16)


def _fox_gate(z, b_pad, n_heads, d):
    bsz, seq, _ = z.shape
    ts = min(GATE_TILE, seq)
    place, ones = _fox_placement(n_heads, d)
    return pl.pallas_call(
        functools.partial(_fox_gate_kernel, n_heads=n_heads),
        grid=(bsz, seq // ts),
        in_specs=[pl.BlockSpec((1, ts, LANES), lambda b, t: (b, t, 0)),
                  pl.BlockSpec((1, LANES), lambda b, t: (0, 0)),
                  pl.BlockSpec((N_SPLIT, LANES, d), lambda b, t: (0, 0, 0)),
                  pl.BlockSpec((1, d), lambda b, t: (0, 0))],
        out_specs=[pl.BlockSpec((1, n_heads, ts), lambda b, t: (b, 0, t)),
                   pl.BlockSpec((1, ts, d), lambda b, t: (b, t, 0))],
        out_shape=[jax.ShapeDtypeStruct((bsz, n_heads, seq), F32),
                   jax.ShapeDtypeStruct((bsz, seq, d), BF16)],
        scratch_shapes=[pltpu.VMEM((1, LANES), F32)],
        compiler_params=_params("parallel", "arbitrary"),
        name="fox_gate",
    )(z, b_pad, place, ones)


def _attn_kernel(*refs, mode, t, tk, n_q, lambda_init):
    if mode == "diff":
        qt_ref, k_ref, vt_ref, lam_ref, subln_ref, o_ref, acc_scr, p_scr, s_scr = refs
    elif mode == "fox":
        qt_ref, k_ref, vt_ref, ex_ref, crow_ref, o_ref, acc_scr, p_scr, s_scr = refs
    else:
        qt_ref, k_ref, vt_ref, o_ref, acc_scr, p_scr, lb_scr, hl_scr = refs
    tiles_per_step = tk // t
    subs = range(tiles_per_step)

    def query_block(qi):
        start = pl.multiple_of(qi * t, t)
        qt = qt_ref[0, :, pl.ds(start, t)]
        feat = lax.broadcasted_iota(jnp.int32, qt.shape, 0)
        zero = jnp.zeros_like(qt)
        q2t = jnp.concatenate([jnp.where(feat < HEAD_DIM, qt, zero),
                               jnp.where(feat >= HEAD_DIM, qt, zero)], axis=1)
        if mode != "fox":
            return q2t
        cq = jnp.concatenate([crow_ref[0, 0, 0:1, pl.ds(start, t)],
                              crow_ref[0, 0, 1:2, pl.ds(start, t)]], axis=1)
        hi, mid, lo = (p.astype(F32) for p in _split3(cq))
        r = lax.broadcasted_iota(jnp.int32, (LANES, 2 * t), 0)
        head_a = lax.broadcasted_iota(jnp.int32, (LANES, 2 * t), 1) < t
        qex = jnp.where(r < N_SPLIT, jnp.where(head_a, 1.0, 0.0),
              jnp.where(r < 2 * N_SPLIT, jnp.where(head_a, 0.0, 1.0),
              jnp.where(r == 2 * N_SPLIT, hi,
              jnp.where(r == 2 * N_SPLIT + 1, mid,
              jnp.where(r == 2 * N_SPLIT + 2, lo, 0.0)))))
        return jnp.concatenate([q2t, qex.astype(BF16)], axis=0)

    def sub(u):
        return pl.ds(u * t, t)

    def causal(qi, u, strict):
        key = lax.broadcasted_iota(jnp.int32, (t, 2 * t), 0) + u * t
        col = lax.broadcasted_iota(jnp.int32, (t, 2 * t), 1)
        query = jnp.where(col >= t, col - t, col) + (qi % tiles_per_step) * t
        return key < query if strict else key <= query

    def scores(q2t, step, u):
        start = pl.multiple_of(step * tk + u * t, t)
        k = k_ref[0, pl.ds(start, t), :]
        if mode == "fox":
            k = jnp.concatenate([k, ex_ref[0, pl.ds(start, t), :]], axis=1)
        return _dot(k, q2t)

    def values(step, rows=V_ROWS):
        return vt_ref[0, 0, :rows, pl.ds(pl.multiple_of(step * tk, tk), tk)]

    def write_pair(qi, ot):
        f = lax.broadcasted_iota(jnp.int32, (LANES, t), 0)
        out = jnp.where(f < HEAD_DIM, ot[:, :t], ot[:, t:]).T
        o_ref[0, pl.ds(pl.multiple_of(qi * t, t), t), :] = out.astype(o_ref.dtype)

    acc_scr[...] = jnp.zeros_like(acc_scr)
    p_scr[...] = jnp.zeros_like(p_scr)

    if mode == "sb":
        shape = (SUM_TILE + ONES_ROWS, SUM_TILE)
        later_key = lax.broadcasted_iota(jnp.int32, shape, 1) > lax.broadcasted_iota(jnp.int32, shape, 0)
        all_keys = lax.broadcasted_iota(jnp.int32, shape, 0) >= SUM_TILE
        later_sum = jnp.where(later_key | all_keys, 1.0, 0.0).astype(BF16)
        later_sum = jnp.concatenate([later_sum, later_sum], axis=1)
        blocks = range(t // SUM_TILE)

        def produce(q2t, qi, step, u, masked):
            z = scores(q2t, step, u)
            neg_abs = lax.bitcast_convert_type(
                lax.bitcast_convert_type(z, jnp.uint32) | jnp.uint32(0x80000000), F32)
            log_beta = jnp.minimum(z, 0.0) - jnp.log(1.0 + jnp.exp2(neg_abs)) * LOG2E
            log_rest = log_beta - z
            if masked:
                valid = causal(qi, u, strict=True)
                log_rest = jnp.where(valid, log_rest, 0.0)
                log_beta = jnp.where(valid, log_beta, -jnp.inf)
            hi = log_rest.astype(BF16)
            lo = (log_rest - hi.astype(F32)).astype(BF16)
            lb_scr[sub(u), :] = log_beta
            pieces = []
            for blk in blocks:
                rows = slice(blk * SUM_TILE, (blk + 1) * SUM_TILE)
                pieces += [hi[rows], lo[rows]]
            hl_scr[pl.ds(2 * u * t, 2 * t), :] = jnp.concatenate(pieces, axis=0)

        def consume(u, rest_sum):
            for blk in reversed(blocks):
                sums = _dot(later_sum, hl_scr[pl.ds(2 * (u * t + blk * SUM_TILE), 2 * SUM_TILE), :])
                rows = pl.ds(u * t + blk * SUM_TILE, SUM_TILE)
                p_scr[rows, :] = jnp.exp2(
                    lb_scr[rows, :] + (sums[:SUM_TILE] + rest_sum)).astype(BF16)
                rest_sum = rest_sum + sums[SUM_TILE:SUM_TILE + 1]
            return rest_sum

        def tile(qi, carry):
            q2t = query_block(qi)
            last_step = qi // tiles_per_step

            def trip(i, carry):
                rest_sum, live = carry
                prev = last_step - jnp.maximum(i - 1, 0)
                acc_scr[...] += live * _dot(values(prev, LANES), p_scr[...])
                for u in reversed(subs):
                    rest_sum = consume(u, rest_sum)
                    produce(q2t, qi, last_step - 1 - i, u, False)
                return rest_sum, jnp.ones_like(live)

            zero_row = jnp.zeros((1, 2 * t), F32)
            rest_sum, live = lax.fori_loop(0, last_step, trip, (zero_row, zero_row))
            nxt = jnp.minimum(qi + 1, n_q - 1)
            q2t_next = query_block(nxt)
            prev = last_step - jnp.maximum(last_step - 1, 0)
            pv_prev = live * _dot(values(prev, LANES), p_scr[...])
            for u in reversed(subs):
                rest_sum = consume(u, rest_sum)
                produce(q2t_next, nxt, nxt // tiles_per_step, u, True)
            ot = acc_scr[...] + pv_prev + _dot(values(0, LANES), p_scr[...])
            acc_scr[...] = jnp.zeros_like(acc_scr)
            write_pair(qi, ot)
            return carry

        q2t = query_block(0)
        for u in subs:
            produce(q2t, 0, 0, u, True)
        lax.fori_loop(0, n_q, tile, 0)
        return

    def produce(q2t, qi, step, u, masked):
        s = scores(q2t, step, u)
        if masked:
            s = jnp.where(causal(qi, u, strict=False), s, -jnp.inf)
        s_scr[sub(u), :] = s
        return jnp.max(s, axis=0, keepdims=True)

    def update(m_new, stage_next):
        m_next = None
        for u in subs:
            p_scr[sub(u), :] = jnp.exp2(s_scr[sub(u), :] - m_new).astype(BF16)
            col_max = stage_next(u)
            m_next = col_max if m_next is None else jnp.maximum(m_next, col_max)
        return m_next

    def tile(qi, m_step):
        q2t = query_block(qi)
        last_step = qi // tiles_per_step

        def step_before(i):
            return jnp.where(i == 1, last_step, jnp.maximum(i - 2, 0))

        def trip(i, carry):
            m_old, m_step, alpha_prev, live = carry
            pv_prev = _dot(values(step_before(i)), p_scr[...])
            acc_scr[...] = alpha_prev * acc_scr[...] + live * pv_prev
            m_new = jnp.maximum(m_old, m_step)
            m_next = update(m_new, lambda u: produce(q2t, qi, i, u, False))
            return m_new, m_next, jnp.exp2(m_old - m_new), jnp.ones_like(live)

        init = (jnp.full((1, 2 * t), -jnp.inf, F32), m_step, jnp.ones((1, 2 * t), F32),
                jnp.zeros((1, 2 * t), F32))
        m_old, m_step, alpha_prev, live = lax.fori_loop(0, last_step, trip, init)
        nxt = jnp.minimum(qi + 1, n_q - 1)
        q2t_next = query_block(nxt)
        pv_prev = live * _dot(values(step_before(last_step)), p_scr[...])
        m_new = jnp.maximum(m_old, m_step)
        m_next = update(m_new, lambda u: produce(q2t_next, nxt, nxt // tiles_per_step, u, True))
        pv = _dot(values(jnp.maximum(last_step - 1, 0)), p_scr[...])
        acc = jnp.exp2(m_old - m_new) * (alpha_prev * acc_scr[...] + pv_prev) + pv
        acc_scr[...] = jnp.zeros_like(acc_scr)
        ot = acc[:LANES] / acc[LANES:LANES + 1]
        if mode == "fox":
            write_pair(qi, ot)
        else:
            lam = lam_ref[...]
            lam_full = (jnp.exp(jnp.sum(lam[0:1] * lam[1:2], axis=-1, keepdims=True))
                        - jnp.exp(jnp.sum(lam[2:3] * lam[3:4], axis=-1, keepdims=True))
                        + lambda_init)
            d = (ot[:, :t] - lam_full * ot[:, t:]).T
            o_ref[0, pl.ds(pl.multiple_of(qi * t, t), t), :] = (
                _rms(d, subln_ref[...]) * (1.0 - lambda_init)).astype(o_ref.dtype)
        return m_next

    q2t = query_block(0)
    m_step = None
    for u in subs:
        col_max = produce(q2t, 0, 0, u, True)
        m_step = col_max if m_step is None else jnp.maximum(m_step, col_max)
    lax.fori_loop(0, n_q, tile, m_step)


def _attention(qt, k, vt, mode, *, extra=(), lambda_init=0.0):
    bsz, d, seq = qt.shape
    t = min(ATTN_Q_TILE, seq)
    tk = min(ATTN_KV_STEP, seq)
    assert seq % tk == 0 and tk % t == 0 and t % SUM_TILE == 0
    token_major = pl.BlockSpec((1, seq, LANES), lambda b, h: (b, 0, h))
    in_specs = [
        pl.BlockSpec((1, LANES, seq), lambda b, h: (b, h, 0)),
        token_major,
        pl.BlockSpec((1, 1, V_ROWS, seq), lambda b, h: (b, h, 0, 0)),
    ]
    if mode == "sb":
        scratch = [pltpu.VMEM((LANES, 2 * t), F32), pltpu.VMEM((tk, 2 * t), BF16),
                   pltpu.VMEM((tk, 2 * t), F32), pltpu.VMEM((2 * tk, 2 * t), BF16)]
    else:
        scratch = [pltpu.VMEM((V_ROWS, 2 * t), F32), pltpu.VMEM((tk, 2 * t), BF16),
                   pltpu.VMEM((tk, 2 * t), F32)]
    if mode == "diff":
        in_specs += [pl.BlockSpec((4, HEAD_DIM), lambda b, h: (0, 0)),
                     pl.BlockSpec((1, LANES), lambda b, h: (0, 0))]
    elif mode == "fox":
        in_specs += [token_major, pl.BlockSpec((1, 1, 2, seq), lambda b, h: (b, h, 0, 0))]
    return pl.pallas_call(
        functools.partial(_attn_kernel, mode=mode, t=t, tk=tk, n_q=seq // t,
                          lambda_init=lambda_init),
        grid=(bsz, d // LANES),
        in_specs=in_specs,
        out_specs=token_major,
        out_shape=jax.ShapeDtypeStruct((bsz, seq, d), BF16),
        scratch_shapes=scratch,
        compiler_params=_params("parallel", "parallel"),
        name="attn_" + mode,
    )(qt, k, vt, *extra)


def _outproj_xattn_kernel(x_ref, o_ref, wo_ref, g_ref, wq_ref, kv_ref, wxo_ref, out_ref):
    d = x_ref.shape[1]
    dh = d // XATTN_HEADS
    x1 = x_ref[...] + _dot(o_ref[...], wo_ref[...])
    h = _rms(x1, g_ref[...]).astype(BF16)
    q = (_dot(h, wq_ref[...]) * (dh ** -0.5)).astype(BF16)
    heads = []
    for hd in range(XATTN_HEADS):
        k = kv_ref[0, :, hd * dh:(hd + 1) * dh]
        v = kv_ref[0, :, d + hd * dh:d + (hd + 1) * dh]
        s = _dot_nt(q[:, hd * dh:(hd + 1) * dh], k)
        p = jnp.exp(s - jnp.max(s, axis=-1, keepdims=True))
        inv = 1.0 / jnp.sum(p, axis=-1, keepdims=True)
        heads.append((_dot(p.astype(BF16), v) * inv).astype(BF16))
    out_ref[...] = x1 + _dot(jnp.concatenate(heads, axis=1), wxo_ref[...])


def _outproj_xattn(x, o, w_o, gain, w_q, kv, w_xo, *, seq):
    m, d = x.shape
    n_mem = kv.shape[1]
    tm = min(ROW_TILE // 2, seq)
    tiles_per_seq = seq // tm
    full = lambda i: (0, 0)
    return pl.pallas_call(
        _outproj_xattn_kernel,
        grid=(m // tm,),
        in_specs=[
            pl.BlockSpec((tm, d), lambda i: (i, 0)),
            pl.BlockSpec((tm, d), lambda i: (i, 0)),
            pl.BlockSpec((d, d), full),
            pl.BlockSpec((1, d), full),
            pl.BlockSpec((d, d), full),
            pl.BlockSpec((1, n_mem, 2 * d), lambda i: (i // tiles_per_seq, 0, 0)),
            pl.BlockSpec((d, d), full),
        ],
        out_specs=pl.BlockSpec((tm, d), lambda i: (i, 0)),
        out_shape=jax.ShapeDtypeStruct((m, d), F32),
        compiler_params=_params("parallel"),
        name="outproj_xattn",
    )(x, o, w_o, gain.reshape(1, d), w_q, kv, w_xo)


def _rope_tables(seq):
    inv = 1.0 / (ROPE_THETA ** (jnp.arange(0, HEAD_DIM, 2, dtype=F32) / HEAD_DIM))
    ang = jnp.arange(seq, dtype=F32)[:, None] * inv[None, :]
    cos, sin = jnp.cos(ang), jnp.sin(ang)
    reps = LANES // HEAD_DIM
    return (cos.T, sin.T,
            jnp.tile(jnp.concatenate([cos, cos], axis=1), (1, reps)),
            jnp.tile(jnp.concatenate([-sin, sin], axis=1), (1, reps)))


def kernel(x, mem, ffn_norm, ffn_w_in, ffn_w_out, mix_norm, diff_w_qkv, diff_lambda, diff_subln, diff_w_o, sb_w_qkv, sb_w_o, fox_w_qkvf, fox_b_f, fox_w_o, xattn_norm, mem_norm, xattn_w_q, xattn_w_kv, xattn_w_o, final_norm):
    bsz, seq, d = x.shape
    n_mem = mem.shape[1]
    depth = ffn_norm.shape[0]
    n_heads = d // HEAD_DIM
    m = bsz * seq
    assert d % LANES == 0 and seq % min(ATTN_Q_TILE, seq) == 0 and seq % min(ROW_TILE, seq) == 0
    rope = _rope_tables(seq)
    qk_scale = HEAD_DIM ** -0.5 * LOG2E
    bf = lambda w: w.astype(BF16)

    xf = x.reshape(m, d)
    memf = mem.reshape(bsz * n_mem, d)
    for i in range(depth):
        mixer, j = i % N_MIXERS, i // N_MIXERS
        xf = _ffn(xf, ffn_norm[i, 0], bf(ffn_w_in[i, 0]), bf(ffn_w_out[i, 0]))

        w_qkv = (diff_w_qkv, sb_w_qkv, fox_w_qkvf)[mixer][j]
        qt, k, vt = _qkv_proj(xf, mix_norm[i], w_qkv[:, :d], w_qkv[:, d:2 * d], w_qkv[:, 2 * d:3 * d],
                              bsz=bsz, seq=seq, scale=qk_scale, rope=rope if mixer == 0 else None)
        k = k.reshape(bsz, seq, d)
        if mixer == 0:
            lambda_init = 0.8 - 0.6 * math.exp(-0.3 * i)
            o = _attention(qt, k, vt, "diff", lambda_init=lambda_init,
                           extra=(diff_lambda[j], diff_subln[j].reshape(1, LANES)))
            w_o = diff_w_o[j]
        elif mixer == 1:
            o = _attention(qt, k, vt, "sb")
            w_o = sb_w_o[j]
        else:
            w_gate = jnp.pad(bf(w_qkv[:, 3 * d:]), ((0, 0), (0, LANES - n_heads)))
            z = _norm_matmul(xf, mix_norm[i], w_gate, out_dtype=F32, tn=LANES)
            b_pad = jnp.pad(fox_b_f[j], (0, LANES - n_heads)).reshape(1, LANES)
            c_row, k_bias = _fox_gate(z.reshape(bsz, seq, LANES), b_pad, n_heads, d)
            o = _attention(qt, k, vt, "fox",
                           extra=(k_bias, c_row.reshape(bsz, n_heads // 2, 2, seq)))
            w_o = fox_w_o[j]

        kv = _norm_matmul(memf, mem_norm, bf(xattn_w_kv[i]), out_dtype=BF16, tn=d)
        xf = _outproj_xattn(xf, o.reshape(m, d), bf(w_o), xattn_norm[i], bf(xattn_w_q[i]),
                            kv.reshape(bsz, n_mem, 2 * d), bf(xattn_w_o[i]), seq=seq)

        xf = _ffn(xf, ffn_norm[i, 1], bf(ffn_w_in[i, 1]), bf(ffn_w_out[i, 1]),
                  final_gain=final_norm if i == depth - 1 else None)
    return xf.reshape(bsz, seq, d)
```

```python
import functools
import math

import numpy as np
import jax
import jax.numpy as jnp
from jax import lax
from jax.experimental import pallas as pl
from jax.experimental.pallas import tpu as pltpu

F32 = jnp.float32
BF16 = jnp.bfloat16

EPS = 1e-5
ROPE_THETA = 10000.0
N_MIXERS = 3
HEAD_DIM = 64
XATTN_HEADS = 4
LANES = 128
VMEM_LIMIT_BYTES = 48 * 1024 * 1024

ROW_TILE = 1024
FFN_TILE = 256
ATTN_Q_TILE = 512
ATTN_KV_STEP = 1024
GATE_TILE = 256
N_SPLIT = 3
ONES_ROWS = 16
SUM_TILE = 128
V_ROWS = LANES + ONES_ROWS
LOG2E = math.log2(math.e)


def _params(*semantics):
    return pltpu.CompilerParams(dimension_semantics=semantics,
                                vmem_limit_bytes=VMEM_LIMIT_BYTES)


def _rms(x, g):
    var = jnp.mean(x * x, axis=-1, keepdims=True)
    return x * lax.rsqrt(var + EPS) * g


def _dot(a, b):
    return jnp.dot(a, b, preferred_element_type=F32)


def _dot_nt(a, b):
    return lax.dot_general(a, b, (((1,), (1,)), ((), ())), preferred_element_type=F32)


def _softplus(z):
    return jnp.maximum(z, 0.0) + jnp.log(1.0 + jnp.exp(-jnp.abs(z)))


def _split3(x):
    hi = x.astype(BF16)
    r1 = x - hi.astype(F32)
    mid = r1.astype(BF16)
    lo = (r1 - mid.astype(F32)).astype(BF16)
    return hi, mid, lo


def _norm_matmul_kernel(x_ref, g_ref, w_ref, o_ref, h_scr):
    @pl.when(pl.program_id(1) == 0)
    def _():
        h_scr[...] = _rms(x_ref[...], g_ref[...]).astype(BF16)

    o_ref[...] = _dot(h_scr[...], w_ref[...]).astype(o_ref.dtype)


def _norm_matmul(x, gain, w, *, out_dtype, tn):
    m, d = x.shape
    n_cols = w.shape[1]
    tm = min(ROW_TILE, m)
    return pl.pallas_call(
        _norm_matmul_kernel,
        grid=(m // tm, n_cols // tn),
        in_specs=[pl.BlockSpec((tm, d), lambda i, n: (i, 0)),
                  pl.BlockSpec((1, d), lambda i, n: (0, 0)),
                  pl.BlockSpec((d, tn), lambda i, n: (0, n))],
        out_specs=pl.BlockSpec((tm, tn), lambda i, n: (i, n)),
        out_shape=jax.ShapeDtypeStruct((m, n_cols), out_dtype),
        scratch_shapes=[pltpu.VMEM((tm, d), BF16)],
        compiler_params=_params("parallel", "arbitrary"),
        name="norm_matmul",
    )(x, gain.reshape(1, d), w)


def _rope_swap(y):
    lane = lax.broadcasted_iota(jnp.int32, y.shape, 1)
    first_half = (lane % HEAD_DIM) < (HEAD_DIM // 2)
    return jnp.where(first_half, pltpu.roll(y, LANES - HEAD_DIM // 2, axis=1),
                     pltpu.roll(y, HEAD_DIM // 2, axis=1))


def _qkv_kernel(x_ref, g_ref, wqt_ref, wk_ref, wvt_ref, *rest, rope, scale):
    if rope:
        cost_ref, sint_ref, cos_ref, sin_ref, qt_ref, k_ref, vt_ref = rest
    else:
        qt_ref, k_ref, vt_ref = rest
    h = _rms(x_ref[...], g_ref[...]).astype(BF16)
    qt = _dot_nt(wqt_ref[...], h) * scale
    k = _dot(h, wk_ref[...])
    if rope:
        half = HEAD_DIM // 2
        ct, st = cost_ref[...], sint_ref[...]
        for r0 in range(0, qt.shape[0], HEAD_DIM):
            x1, x2 = qt[r0:r0 + half], qt[r0 + half:r0 + HEAD_DIM]
            qt_ref[0, r0:r0 + half, :] = (x1 * ct - x2 * st).astype(BF16)
            qt_ref[0, r0 + half:r0 + HEAD_DIM, :] = (x2 * ct + x1 * st).astype(BF16)
        cos, sin = cos_ref[...], sin_ref[...]
        for c0 in range(0, k.shape[1], LANES):
            kc = k[:, c0:c0 + LANES]
            k_ref[:, c0:c0 + LANES] = (kc * cos + _rope_swap(kc) * sin).astype(BF16)
    else:
        qt_ref[0] = qt.astype(BF16)
        k_ref[...] = k.astype(BF16)
    vt = _dot_nt(wvt_ref[...], h)
    for pair in range(vt_ref.shape[1]):
        vt_ref[0, pair, :LANES, :] = vt[pair * LANES:(pair + 1) * LANES].astype(BF16)
        vt_ref[0, pair, LANES:, :] = jnp.ones((ONES_ROWS, vt.shape[1]), BF16)


def _qkv_proj(x, gain, w_q, w_k, w_v, *, bsz, seq, scale, rope=None):
    m, d = x.shape
    tm = min(ROW_TILE // 2, seq)
    tiles_per_seq = seq // tm
    n_pairs = d // LANES
    full = lambda i: (0, 0)
    feat_major = pl.BlockSpec((1, d, tm), lambda i: (i // tiles_per_seq, 0, i % tiles_per_seq))
    values_spec = pl.BlockSpec((1, n_pairs, V_ROWS, tm),
                               lambda i: (i // tiles_per_seq, 0, 0, i % tiles_per_seq))
    in_specs = [pl.BlockSpec((tm, d), lambda i: (i, 0)),
                pl.BlockSpec((1, d), full),
                pl.BlockSpec((d, d), full), pl.BlockSpec((d, d), full), pl.BlockSpec((d, d), full)]
    args = [x, gain.reshape(1, d), w_q.T.astype(BF16), w_k.astype(BF16), w_v.T.astype(BF16)]
    if rope is not None:
        half = HEAD_DIM // 2
        in_specs += [pl.BlockSpec((half, tm), lambda i: (0, i % tiles_per_seq))] * 2
        in_specs += [pl.BlockSpec((tm, LANES), lambda i: (i % tiles_per_seq, 0))] * 2
        args += list(rope)
    return pl.pallas_call(
        functools.partial(_qkv_kernel, rope=rope is not None, scale=scale),
        grid=(m // tm,),
        in_specs=in_specs,
        out_specs=[feat_major, pl.BlockSpec((tm, d), lambda i: (i, 0)), values_spec],
        out_shape=[jax.ShapeDtypeStruct((bsz, d, seq), BF16),
                   jax.ShapeDtypeStruct((m, d), BF16),
                   jax.ShapeDtypeStruct((bsz, n_pairs, V_ROWS, seq), BF16)],
        compiler_params=_params("parallel"),
        name="qkv_proj",
    )(*args)


def _ffn_kernel(x_ref, g_ref, win_ref, wo_ref, *rest, final_norm):
    if final_norm:
        fg_ref, o_ref, act_scr = rest
    else:
        o_ref, act_scr = rest
    d_ff = wo_ref.shape[0]
    x = x_ref[...]
    h = _rms(x, g_ref[...]).astype(BF16)
    for c0 in range(0, d_ff, FFN_TILE):
        gate = _dot(h, win_ref[:, c0:c0 + FFN_TILE])
        up = _dot(h, win_ref[:, d_ff + c0:d_ff + c0 + FFN_TILE])
        act_scr[:, c0:c0 + FFN_TILE] = (gate * jax.nn.sigmoid(gate) * up).astype(BF16)
    y = x + 0.5 * _dot(act_scr[...], wo_ref[...])
    if final_norm:
        y = _rms(y, fg_ref[...])
    o_ref[...] = y


def _ffn(x, gain, w_in, w_out, final_gain=None):
    m, d = x.shape
    d_ff = w_out.shape[0]
    tm = min(ROW_TILE // 2, m)
    full = lambda i: (0, 0)
    resident = pl.Buffered(1)
    in_specs = [
        pl.BlockSpec((tm, d), lambda i: (i, 0)),
        pl.BlockSpec((1, d), full),
        pl.BlockSpec((d, 2 * d_ff), full, pipeline_mode=resident),
        pl.BlockSpec((d_ff, d), full, pipeline_mode=resident),
    ]
    args = [x, gain.reshape(1, d), w_in, w_out]
    if final_gain is not None:
        in_specs.append(pl.BlockSpec((1, d), full))
        args.append(final_gain.reshape(1, d))
    return pl.pallas_call(
        functools.partial(_ffn_kernel, final_norm=final_gain is not None),
        grid=(m // tm,),
        in_specs=in_specs,
        out_specs=pl.BlockSpec((tm, d), lambda i: (i, 0)),
        out_shape=jax.ShapeDtypeStruct((m, d), F32),
        scratch_shapes=[pltpu.VMEM((tm, d_ff), BF16)],
        compiler_params=_params("parallel"),
        name="ffn",
    )(*args)


def _fox_placement(n_heads, d):
    place = np.zeros((N_SPLIT, LANES, d), np.float32)
    ones = np.zeros((1, d), np.float32)
    for h in range(n_heads):
        base = (h // 2) * LANES + (h % 2) * N_SPLIT
        for piece in range(N_SPLIT):
            place[piece, h, base + piece] = -1.0
    for pair in range(d // LANES):
        ones[0, pair * LANES + 2 * N_SPLIT:pair * LANES + 3 * N_SPLIT] = 1.0
    return jnp.asarray(place, BF16), jnp.asarray(ones, F32)


def _fox_gate_kernel(z_ref, b_ref, place_ref, ones_ref, crow_ref, ex_ref, carry_scr, *, n_heads):
    @pl.when(pl.program_id(1) == 0)
    def _():
        carry_scr[...] = jnp.zeros_like(carry_scr)

    z = z_ref[0] + b_ref[...]
    log_f = -_softplus(-z) * LOG2E
    ts = log_f.shape[0]
    row = lax.broadcasted_iota(jnp.int32, (ts, ts), 0)
    col = lax.broadcasted_iota(jnp.int32, (ts, ts), 1)
    tri = jnp.where(row >= col, 1.0, 0.0).astype(BF16)
    hi, mid, lo = _split3(log_f)
    c = _dot(tri, hi) + _dot(tri, mid) + _dot(tri, lo) + carry_scr[...]
    carry_scr[...] = c[ts - 1:ts, :]
    crow_ref[0] = c.T[:n_heads]
    hi, mid, lo = _split3(c)
    ex = _dot(hi, place_ref[0]) + _dot(mid, place_ref[1]) + _dot(lo, place_ref[2]) + ones_ref[...]
    ex_ref[0] = ex.astype(BF16)


def _fox_gate(z, b_pad, n_heads, d):
    bsz, seq, _ = z.shape
    ts = min(GATE_TILE, seq)
    place, ones = _fox_placement(n_heads, d)
    return pl.pallas_call(
        functools.partial(_fox_gate_kernel, n_heads=n_heads),
        grid=(bsz, seq // ts),
        in_specs=[pl.BlockSpec((1, ts, LANES), lambda b, t: (b, t, 0)),
                  pl.BlockSpec((1, LANES), lambda b, t: (0, 0)),
                  pl.BlockSpec((N_SPLIT, LANES, d), lambda b, t: (0, 0, 0)),
                  pl.BlockSpec((1, d), lambda b, t: (0, 0))],
        out_specs=[pl.BlockSpec((1, n_heads, ts), lambda b, t: (b, 0, t)),
                   pl.BlockSpec((1, ts, d), lambda b, t: (b, t, 0))],
        out_shape=[jax.ShapeDtypeStruct((bsz, n_heads, seq), F32),
                   jax.ShapeDtypeStruct((bsz, seq, d), BF16)],
        scratch_shapes=[pltpu.VMEM((1, LANES), F32)],
        compiler_params=_params("parallel", "arbitrary"),
        name="fox_gate",
    )(z, b_pad, place, ones)


def _attn_kernel(*refs, mode, t, tk, n_q, lambda_init):
    if mode == "diff":
        qt_ref, k_ref, vt_ref, lam_ref, subln_ref, o_ref, acc_scr, p_scr, s_scr = refs
    elif mode == "fox":
        qt_ref, k_ref, vt_ref, ex_ref, crow_ref, o_ref, acc_scr, p_scr, s_scr = refs
    else:
        qt_ref, k_ref, vt_ref, o_ref, acc_scr, p_scr, lb_scr, hl_scr = refs
    tiles_per_step = tk // t
    subs = range(tiles_per_step)

    def query_block(qi):
        start = pl.multiple_of(qi * t, t)
        qt = qt_ref[0, :, pl.ds(start, t)]
        feat = lax.broadcasted_iota(jnp.int32, qt.shape, 0)
        zero = jnp.zeros_like(qt)
        q2t = jnp.concatenate([jnp.where(feat < HEAD_DIM, qt, zero),
                               jnp.where(feat >= HEAD_DIM, qt, zero)], axis=1)
        if mode != "fox":
            return q2t
        cq = jnp.concatenate([crow_ref[0, 0, 0:1, pl.ds(start, t)],
                              crow_ref[0, 0, 1:2, pl.ds(start, t)]], axis=1)
        hi, mid, lo = (p.astype(F32) for p in _split3(cq))
        r = lax.broadcasted_iota(jnp.int32, (LANES, 2 * t), 0)
        head_a = lax.broadcasted_iota(jnp.int32, (LANES, 2 * t), 1) < t
        qex = jnp.where(r < N_SPLIT, jnp.where(head_a, 1.0, 0.0),
              jnp.where(r < 2 * N_SPLIT, jnp.where(head_a, 0.0, 1.0),
              jnp.where(r == 2 * N_SPLIT, hi,
              jnp.where(r == 2 * N_SPLIT + 1, mid,
              jnp.where(r == 2 * N_SPLIT + 2, lo, 0.0)))))
        return jnp.concatenate([q2t, qex.astype(BF16)], axis=0)

    def sub(u):
        return pl.ds(u * t, t)

    def causal(qi, u, strict):
        key = lax.broadcasted_iota(jnp.int32, (t, 2 * t), 0) + u * t
        col = lax.broadcasted_iota(jnp.int32, (t, 2 * t), 1)
        query = jnp.where(col >= t, col - t, col) + (qi % tiles_per_step) * t
        return key < query if strict else key <= query

    def scores(q2t, step, u):
        start = pl.multiple_of(step * tk + u * t, t)
        k = k_ref[0, pl.ds(start, t), :]
        if mode == "fox":
            k = jnp.concatenate([k, ex_ref[0, pl.ds(start, t), :]], axis=1)
        return _dot(k, q2t)

    def values(step, rows=V_ROWS):
        return vt_ref[0, 0, :rows, pl.ds(pl.multiple_of(step * tk, tk), tk)]

    def write_pair(qi, ot):
        f = lax.broadcasted_iota(jnp.int32, (LANES, t), 0)
        out = jnp.where(f < HEAD_DIM, ot[:, :t], ot[:, t:]).T
        o_ref[0, pl.ds(pl.multiple_of(qi * t, t), t), :] = out.astype(o_ref.dtype)

    acc_scr[...] = jnp.zeros_like(acc_scr)
    p_scr[...] = jnp.zeros_like(p_scr)

    if mode == "sb":
        shape = (SUM_TILE + ONES_ROWS, SUM_TILE)
        later_key = lax.broadcasted_iota(jnp.int32, shape, 1) > lax.broadcasted_iota(jnp.int32, shape, 0)
        all_keys = lax.broadcasted_iota(jnp.int32, shape, 0) >= SUM_TILE
        later_sum = jnp.where(later_key | all_keys, 1.0, 0.0).astype(BF16)
        later_sum = jnp.concatenate([later_sum, later_sum], axis=1)
        blocks = range(t // SUM_TILE)

        def produce(q2t, qi, step, u, masked):
            z = scores(q2t, step, u)
            neg_abs = lax.bitcast_convert_type(
                lax.bitcast_convert_type(z, jnp.uint32) | jnp.uint32(0x80000000), F32)
            log_beta = jnp.minimum(z, 0.0) - jnp.log(1.0 + jnp.exp2(neg_abs)) * LOG2E
            log_rest = log_beta - z
            if masked:
                valid = causal(qi, u, strict=True)
                log_rest = jnp.where(valid, log_rest, 0.0)
                log_beta = jnp.where(valid, log_beta, -jnp.inf)
            hi = log_rest.astype(BF16)
            lo = (log_rest - hi.astype(F32)).astype(BF16)
            lb_scr[sub(u), :] = log_beta
            pieces = []
            for blk in blocks:
                rows = slice(blk * SUM_TILE, (blk + 1) * SUM_TILE)
                pieces += [hi[rows], lo[rows]]
            hl_scr[pl.ds(2 * u * t, 2 * t), :] = jnp.concatenate(pieces, axis=0)

        def consume(u, rest_sum):
            for blk in reversed(blocks):
                sums = _dot(later_sum, hl_scr[pl.ds(2 * (u * t + blk * SUM_TILE), 2 * SUM_TILE), :])
                rows = pl.ds(u * t + blk * SUM_TILE, SUM_TILE)
                p_scr[rows, :] = jnp.exp2(
                    lb_scr[rows, :] + (sums[:SUM_TILE] + rest_sum)).astype(BF16)
                rest_sum = rest_sum + sums[SUM_TILE:SUM_TILE + 1]
            return rest_sum

        def tile(qi, carry):
            q2t = query_block(qi)
            last_step = qi // tiles_per_step

            def trip(i, carry):
                rest_sum, live = carry
                prev = last_step - jnp.maximum(i - 1, 0)
                acc_scr[...] += live * _dot(values(prev, LANES), p_scr[...])
                for u in reversed(subs):
                    rest_sum = consume(u, rest_sum)
                    produce(q2t, qi, last_step - 1 - i, u, False)
                return rest_sum, jnp.ones_like(live)

            zero_row = jnp.zeros((1, 2 * t), F32)
            rest_sum, live = lax.fori_loop(0, last_step, trip, (zero_row, zero_row))
            nxt = jnp.minimum(qi + 1, n_q - 1)
            q2t_next = query_block(nxt)
            prev = last_step - jnp.maximum(last_step - 1, 0)
            pv_prev = live * _dot(values(prev, LANES), p_scr[...])
            for u in reversed(subs):
                rest_sum = consume(u, rest_sum)
                produce(q2t_next, nxt, nxt // tiles_per_step, u, True)
            ot = acc_scr[...] + pv_prev + _dot(values(0, LANES), p_scr[...])
            acc_scr[...] = jnp.zeros_like(acc_scr)
            write_pair(qi, ot)
            return carry

        q2t = query_block(0)
        for u in subs:
            produce(q2t, 0, 0, u, True)
        lax.fori_loop(0, n_q, tile, 0)
        return

    def produce(q2t, qi, step, u, masked):
        s = scores(q2t, step, u)
        if masked:
            s = jnp.where(causal(qi, u, strict=False), s, -jnp.inf)
        s_scr[sub(u), :] = s
        return jnp.max(s, axis=0, keepdims=True)

    def update(m_new, stage_next):
        m_next = None
        for u in subs:
            p_scr[sub(u), :] = jnp.exp2(s_scr[sub(u), :] - m_new).astype(BF16)
            col_max = stage_next(u)
            m_next = col_max if m_next is None else jnp.maximum(m_next, col_max)
        return m_next

    def tile(qi, m_step):
        q2t = query_block(qi)
        last_step = qi // tiles_per_step

        def step_before(i):
            return jnp.where(i == 1, last_step, jnp.maximum(i - 2, 0))

        def trip(i, carry):
            m_old, m_step, alpha_prev, live = carry
            pv_prev = _dot(values(step_before(i)), p_scr[...])
            acc_scr[...] = alpha_prev * acc_scr[...] + live * pv_prev
            m_new = jnp.maximum(m_old, m_step)
            m_next = update(m_new, lambda u: produce(q2t, qi, i, u, False))
            return m_new, m_next, jnp.exp2(m_old - m_new), jnp.ones_like(live)

        init = (jnp.full((1, 2 * t), -jnp.inf, F32), m_step, jnp.ones((1, 2 * t), F32),
                jnp.zeros((1, 2 * t), F32))
        m_old, m_step, alpha_prev, live = lax.fori_loop(0, last_step, trip, init)
        nxt = jnp.minimum(qi + 1, n_q - 1)
        q2t_next = query_block(nxt)
        pv_prev = live * _dot(values(step_before(last_step)), p_scr[...])
        m_new = jnp.maximum(m_old, m_step)
        m_next = update(m_new, lambda u: produce(q2t_next, nxt, nxt // tiles_per_step, u, True))
        pv = _dot(values(jnp.maximum(last_step - 1, 0)), p_scr[...])
        acc = jnp.exp2(m_old - m_new) * (alpha_prev * acc_scr[...] + pv_prev) + pv
        acc_scr[...] = jnp.zeros_like(acc_scr)
        ot = acc[:LANES] / acc[LANES:LANES + 1]
        if mode == "fox":
            write_pair(qi, ot)
        else:
            lam = lam_ref[...]
            lam_full = (jnp.exp(jnp.sum(lam[0:1] * lam[1:2], axis=-1, keepdims=True))
                        - jnp.exp(jnp.sum(lam[2:3] * lam[3:4], axis=-1, keepdims=True))
                        + lambda_init)
            d = (ot[:, :t] - lam_full * ot[:, t:]).T
            o_ref[0, pl.ds(pl.multiple_of(qi * t, t), t), :] = (
                _rms(d, subln_ref[...]) * (1.0 - lambda_init)).astype(o_ref.dtype)
        return m_next

    q2t = query_block(0)
    m_step = None
    for u in subs:
        col_max = produce(q2t, 0, 0, u, True)
        m_step = col_max if m_step is None else jnp.maximum(m_step, col_max)
    lax.fori_loop(0, n_q, tile, m_step)


def _attention(qt, k, vt, mode, *, extra=(), lambda_init=0.0):
    bsz, d, seq = qt.shape
    t = min(ATTN_Q_TILE, seq)
    tk = min(ATTN_KV_STEP, seq)
    assert seq % tk == 0 and tk % t == 0 and t % SUM_TILE == 0
    token_major = pl.BlockSpec((1, seq, LANES), lambda b, h: (b, 0, h))
    in_specs = [
        pl.BlockSpec((1, LANES, seq), lambda b, h: (b, h, 0)),
        token_major,
        pl.BlockSpec((1, 1, V_ROWS, seq), lambda b, h: (b, h, 0, 0)),
    ]
    if mode == "sb":
        scratch = [pltpu.VMEM((LANES, 2 * t), F32), pltpu.VMEM((tk, 2 * t), BF16),
                   pltpu.VMEM((tk, 2 * t), F32), pltpu.VMEM((2 * tk, 2 * t), BF16)]
    else:
        scratch = [pltpu.VMEM((V_ROWS, 2 * t), F32), pltpu.VMEM((tk, 2 * t), BF16),
                   pltpu.VMEM((tk, 2 * t), F32)]
    if mode == "diff":
        in_specs += [pl.BlockSpec((4, HEAD_DIM), lambda b, h: (0, 0)),
                     pl.BlockSpec((1, LANES), lambda b, h: (0, 0))]
    elif mode == "fox":
        in_specs += [token_major, pl.BlockSpec((1, 1, 2, seq), lambda b, h: (b, h, 0, 0))]
    return pl.pallas_call(
        functools.partial(_attn_kernel, mode=mode, t=t, tk=tk, n_q=seq // t,
                          lambda_init=lambda_init),
        grid=(bsz, d // LANES),
        in_specs=in_specs,
        out_specs=token_major,
        out_shape=jax.ShapeDtypeStruct((bsz, seq, d), BF16),
        scratch_shapes=scratch,
        compiler_params=_params("parallel", "parallel"),
        name="attn_" + mode,
    )(qt, k, vt, *extra)


def _outproj_xattn_kernel(x_ref, o_ref, wo_ref, g_ref, wq_ref, kv_ref, wxo_ref, out_ref):
    d = x_ref.shape[1]
    dh = d // XATTN_HEADS
    x1 = x_ref[...] + _dot(o_ref[...], wo_ref[...])
    h = _rms(x1, g_ref[...]).astype(BF16)
    q = (_dot(h, wq_ref[...]) * (dh ** -0.5)).astype(BF16)
    heads = []
    for hd in range(XATTN_HEADS):
        k = kv_ref[0, :, hd * dh:(hd + 1) * dh]
        v = kv_ref[0, :, d + hd * dh:d + (hd + 1) * dh]
        s = _dot_nt(q[:, hd * dh:(hd + 1) * dh], k)
        p = jnp.exp(s - jnp.max(s, axis=-1, keepdims=True))
        inv = 1.0 / jnp.sum(p, axis=-1, keepdims=True)
        heads.append((_dot(p.astype(BF16), v) * inv).astype(BF16))
    out_ref[...] = x1 + _dot(jnp.concatenate(heads, axis=1), wxo_ref[...])


def _outproj_xattn(x, o, w_o, gain, w_q, kv, w_xo, *, seq):
    m, d = x.shape
    n_mem = kv.shape[1]
    tm = min(ROW_TILE // 2, seq)
    tiles_per_seq = seq // tm
    full = lambda i: (0, 0)
    return pl.pallas_call(
        _outproj_xattn_kernel,
        grid=(m // tm,),
        in_specs=[
            pl.BlockSpec((tm, d), lambda i: (i, 0)),
            pl.BlockSpec((tm, d), lambda i: (i, 0)),
            pl.BlockSpec((d, d), full),
            pl.BlockSpec((1, d), full),
            pl.BlockSpec((d, d), full),
            pl.BlockSpec((1, n_mem, 2 * d), lambda i: (i // tiles_per_seq, 0, 0)),
            pl.BlockSpec((d, d), full),
        ],
        out_specs=pl.BlockSpec((tm, d), lambda i: (i, 0)),
        out_shape=jax.ShapeDtypeStruct((m, d), F32),
        compiler_params=_params("parallel"),
        name="outproj_xattn",
    )(x, o, w_o, gain.reshape(1, d), w_q, kv, w_xo)


def _rope_tables(seq):
    inv = 1.0 / (ROPE_THETA ** (jnp.arange(0, HEAD_DIM, 2, dtype=F32) / HEAD_DIM))
    ang = jnp.arange(seq, dtype=F32)[:, None] * inv[None, :]
    cos, sin = jnp.cos(ang), jnp.sin(ang)
    reps = LANES // HEAD_DIM
    return (cos.T, sin.T,
            jnp.tile(jnp.concatenate([cos, cos], axis=1), (1, reps)),
            jnp.tile(jnp.concatenate([-sin, sin], axis=1), (1, reps)))


def kernel(x, mem, ffn_norm, ffn_w_in, ffn_w_out, mix_norm, diff_w_qkv, diff_lambda, diff_subln, diff_w_o, sb_w_qkv, sb_w_o, fox_w_qkvf, fox_b_f, fox_w_o, xattn_norm, mem_norm, xattn_w_q, xattn_w_kv, xattn_w_o, final_norm):
    bsz, seq, d = x.shape
    n_mem = mem.shape[1]
    depth = ffn_norm.shape[0]
    n_heads = d // HEAD_DIM
    m = bsz * seq
    assert d % LANES == 0 and seq % min(ATTN_Q_TILE, seq) == 0 and seq % min(ROW_TILE, seq) == 0
    rope = _rope_tables(seq)
    qk_scale = HEAD_DIM ** -0.5 * LOG2E
    bf = lambda w: w.astype(BF16)

    xf = x.reshape(m, d)
    memf = mem.reshape(bsz * n_mem, d)
    for i in range(depth):
        mixer, j = i % N_MIXERS, i // N_MIXERS
        xf = _ffn(xf, ffn_norm[i, 0], bf(ffn_w_in[i, 0]), bf(ffn_w_out[i, 0]))

        w_qkv = (diff_w_qkv, sb_w_qkv, fox_w_qkvf)[mixer][j]
        qt, k, vt = _qkv_proj(xf, mix_norm[i], w_qkv[:, :d], w_qkv[:, d:2 * d], w_qkv[:, 2 * d:3 * d],
                              bsz=bsz, seq=seq, scale=qk_scale, rope=rope if mixer == 0 else None)
        k = k.reshape(bsz, seq, d)
        if mixer == 0:
            lambda_init = 0.8 - 0.6 * math.exp(-0.3 * i)
            o = _attention(qt, k, vt, "diff", lambda_init=lambda_init,
                           extra=(diff_lambda[j], diff_subln[j].reshape(1, LANES)))
            w_o = diff_w_o[j]
        elif mixer == 1:
            o = _attention(qt, k, vt, "sb")
            w_o = sb_w_o[j]
        else:
            w_gate = jnp.pad(bf(w_qkv[:, 3 * d:]), ((0, 0), (0, LANES - n_heads)))
            z = _norm_matmul(xf, mix_norm[i], w_gate, out_dtype=F32, tn=LANES)
            b_pad = jnp.pad(fox_b_f[j], (0, LANES - n_heads)).reshape(1, LANES)
            c_row, k_bias = _fox_gate(z.reshape(bsz, seq, LANES), b_pad, n_heads, d)
            o = _attention(qt, k, vt, "fox",
                           extra=(k_bias, c_row.reshape(bsz, n_heads // 2, 2, seq)))
            w_o = fox_w_o[j]

        kv = _norm_matmul(memf, mem_norm, bf(xattn_w_kv[i]), out_dtype=BF16, tn=d)
        xf = _outproj_xattn(xf, o.reshape(m, d), bf(w_o), xattn_norm[i], bf(xattn_w_q[i]),
                            kv.reshape(bsz, n_mem, 2 * d), bf(xattn_w_o[i]), seq=seq)

        xf = _ffn(xf, ffn_norm[i, 1], bf(ffn_w_in[i, 1]), bf(ffn_w_out[i, 1]),
                  final_gain=final_norm if i == depth - 1 else None)
    return xf.reshape(bsz, seq, d)
```
